```python
import jax, jax.numpy as jnp
from jax import lax
import numpy as np

D_MODEL = 1024
BATCH = 8
SEQ = 4096
DEPTH = 2

G_GROUPS = 4
G_GROUP_DIM = 128
G_WIDTH = G_GROUPS * G_GROUP_DIM
G_CHUNK = 128
R_HEADS = 4
R_DK = 128
R_DV = 128
R_WIDTH = R_HEADS * R_DK
R_CHUNK = 64
A_HEADS = 8
A_HEAD_DIM = 64
A_WIDTH = A_HEADS * A_HEAD_DIM
A_BLOCK = 128
N_BRANCH = 3
D_FF = -(-8 * D_MODEL // (3 * 256)) * 256
LN_EPS = 1e-5
RMS_EPS = 1e-6
NEG_BIG = -1e30
LB_FLOOR = 1e-30
IN_SPLITS = (G_WIDTH, G_WIDTH, R_WIDTH, R_WIDTH, R_WIDTH, R_WIDTH,
             A_WIDTH, A_WIDTH, A_WIDTH, A_HEADS, N_BRANCH * D_MODEL)
IN_COLS = 2 * G_WIDTH + 4 * R_WIDTH + 3 * A_WIDTH + A_HEADS + N_BRANCH * D_MODEL

kernel_name = "hybrid_gmlp_hgrn2_fox_deepnorm"


def layer_norm(x, g, b):
    x32 = x.astype(jnp.float32)
    mu = jnp.mean(x32, axis=-1, keepdims=True)
    var = jnp.mean(jnp.square(x32 - mu), axis=-1, keepdims=True)
    y = (x32 - mu) * lax.rsqrt(var + LN_EPS)
    return (y * g.astype(jnp.float32) + b.astype(jnp.float32)).astype(x.dtype)


def gmlp_branch(u, v, ln_g, ln_b, w_s, b_s):
    B, S, _ = v.shape
    v = layer_norm(v, ln_g, ln_b)
    vc = v.reshape(B, S // G_CHUNK, G_CHUNK, G_GROUPS, G_GROUP_DIM)
    causal = jnp.tril(jnp.ones((G_CHUNK, G_CHUNK), dtype=bool))
    w = jnp.where(causal[None], w_s, 0)
    sv = jnp.einsum('gts,bnsgc->bntgc', w, vc) + b_s.T[None, None, :, :, None]
    return u * sv.reshape(B, S, G_WIDTH)


def to_chunks(t, heads, dim, chunk):
    B, S, _ = t.shape
    return t.reshape(B, S // chunk, chunk, heads, dim).transpose(1, 0, 3, 2, 4)


def hgrn2_branch(q, f_logit, i_in, g, lb, norm_w):
    B, S, _ = q.shape
    f32 = jnp.float32
    z = f_logit.astype(f32)
    lb = lb.astype(f32)
    log_lb = jnp.log(jnp.maximum(lb, LB_FLOOR))
    log_f = jnp.logaddexp(log_lb, jnp.log1p(-lb) + jax.nn.log_sigmoid(z))
    k = (1.0 - lb) * jax.nn.sigmoid(-z)
    qc = to_chunks(q.astype(f32), R_HEADS, R_DK, R_CHUNK)
    kc = to_chunks(k, R_HEADS, R_DK, R_CHUNK)
    vc = to_chunks(i_in.astype(f32), R_HEADS, R_DV, R_CHUNK)
    lfc = to_chunks(log_f, R_HEADS, R_DK, R_CHUNK)
    causal = jnp.tril(jnp.ones((R_CHUNK, R_CHUNK), dtype=bool))

    def step(state, inp):
        qb, kb, vb, lfb = inp
        cum = jnp.cumsum(lfb, axis=2)
        inter = jnp.einsum('bhtk,bhkv->bhtv', qb * jnp.exp(cum), state)
        diff = cum[:, :, :, None, :] - cum[:, :, None, :, :]
        decay = jnp.exp(jnp.where(causal[:, :, None], diff, NEG_BIG))
        scores = jnp.einsum('bhtk,bhtsk,bhsk->bhts', qb, decay, kb)
        intra = jnp.einsum('bhts,bhsv->bhtv', scores, vb)
        last = cum[:, :, -1:, :]
        new_state = (jnp.exp(last[:, :, 0, :])[..., None] * state
                     + jnp.einsum('bhsk,bhsv->bhkv', kb * jnp.exp(last - cum), vb))
        return new_state, inter + intra

    state0 = jnp.zeros((B, R_HEADS, R_DK, R_DV), f32)
    _, o = lax.scan(step, state0, (qc, kc, vc, lfc))
    o = o.transpose(1, 0, 3, 2, 4).reshape(B, S, R_HEADS, R_DV)
    gg = g.astype(f32).reshape(B, S, R_HEADS, R_DV)
    o = o * lax.rsqrt(jnp.mean(jnp.square(o), axis=-1, keepdims=True) + RMS_EPS)
    o = o * norm_w.astype(f32) * jax.nn.silu(gg)
    return o.reshape(B, S, R_WIDTH).astype(q.dtype)


def fox_branch(q, k, v, f_logit, f_bias):
    B, S, _ = q.shape
    nb = S // A_BLOCK
    scale = A_HEAD_DIM ** -0.5
    qh = q.reshape(B, S, A_HEADS, A_HEAD_DIM).transpose(0, 2, 1, 3)
    kh = k.reshape(B, S, A_HEADS, A_HEAD_DIM).transpose(0, 2, 1, 3)
    vh = v.reshape(B, S, A_HEADS, A_HEAD_DIM).transpose(0, 2, 1, 3)
    log_f = jax.nn.log_sigmoid(f_logit.astype(jnp.float32) + f_bias.astype(jnp.float32))
    F = jnp.cumsum(log_f, axis=1).transpose(0, 2, 1)
    q_blocks = qh.reshape(B, A_HEADS, nb, A_BLOCK, A_HEAD_DIM).transpose(2, 0, 1, 3, 4)
    F_blocks = F.reshape(B, A_HEADS, nb, A_BLOCK).transpose(2, 0, 1, 3)
    k_pos = jnp.arange(S)

    def block(args):
        qb, Fb, idx = args
        q_pos = idx * A_BLOCK + jnp.arange(A_BLOCK)
        s = (jnp.einsum('bhtd,bhsd->bhts', qb, kh).astype(jnp.float32) * scale
             + Fb[..., None] - F[:, :, None, :])
        s = jnp.where(k_pos[None, :] <= q_pos[:, None], s, NEG_BIG)
        p = jax.nn.softmax(s, axis=-1)
        return jnp.einsum('bhts,bhsd->bhtd', p.astype(vh.dtype), vh)

    out = lax.map(block, (q_blocks, F_blocks, jnp.arange(nb)))
    return out.transpose(1, 0, 3, 2, 4).reshape(B, S, A_WIDTH)


def mixer_block(x, w_in, g_ln_g, g_ln_b, g_ws, g_bs, r_lb, r_norm_w, a_fb,
                w_pg, w_pr, w_pa, w_o):
    B, S, _ = x.shape
    h = jnp.einsum('bsd,dc->bsc', x, w_in)
    offsets = [int(o) for o in np.cumsum(IN_SPLITS)[:-1]]
    (gu, gv, rq, rf, ri, rg, aq, ak, av, af, gate_logits) = jnp.split(h, offsets, axis=-1)
    y_g = gmlp_branch(jax.nn.gelu(gu, approximate=False), jax.nn.gelu(gv, approximate=False),
                      g_ln_g, g_ln_b, g_ws, g_bs)
    y_r = hgrn2_branch(rq, rf, ri, rg, r_lb, r_norm_w)
    y_a = fox_branch(aq, ak, av, af, a_fb)
    gates = jax.nn.sigmoid(gate_logits).reshape(B, S, N_BRANCH, D_MODEL)
    m = (gates[:, :, 0] * jnp.einsum('bsc,cd->bsd', y_g, w_pg)
         + gates[:, :, 1] * jnp.einsum('bsc,cd->bsd', y_r, w_pr)
         + gates[:, :, 2] * jnp.einsum('bsc,cd->bsd', y_a, w_pa))
    return jnp.einsum('bsd,de->bse', m, w_o)


def swiglu(x, w_gate, w_up, w_down):
    hid = jax.nn.silu(jnp.einsum('bsd,df->bsf', x, w_gate)) * jnp.einsum('bsd,df->bsf', x, w_up)
    return jnp.einsum('bsf,fd->bsd', hid, w_down)


def setup_inputs(seed: int = 0) -> dict:
    key = jax.random.key(seed)
    ks = jax.random.split(key, 24)
    beta = (8 * DEPTH) ** -0.25
    n = lambda k, shape: jax.random.normal(k, shape, jnp.float32)
    return {
        "x": n(ks[0], (BATCH, SEQ, D_MODEL)),
        "w_in": n(ks[1], (DEPTH, D_MODEL, IN_COLS)) * D_MODEL ** -0.5,
        "gmlp_ln_g": 1.0 + 0.01 * n(ks[2], (DEPTH, G_WIDTH)),
        "gmlp_ln_b": 0.01 * n(ks[3], (DEPTH, G_WIDTH)),
        "gmlp_ws": n(ks[4], (DEPTH, G_GROUPS, G_CHUNK, G_CHUNK)) * G_CHUNK ** -0.5,
        "gmlp_bs": 1.0 + 0.01 * n(ks[5], (DEPTH, G_GROUPS, G_CHUNK)),
        "hgrn_lb_logits": n(ks[6], (DEPTH, R_WIDTH)),
        "hgrn_norm_w": 1.0 + 0.01 * n(ks[7], (DEPTH, R_DV)),
        "fox_fb": 2.0 + 0.1 * n(ks[8], (DEPTH, A_HEADS)),
        "w_pg": n(ks[9], (DEPTH, G_WIDTH, D_MODEL)) * G_WIDTH ** -0.5 * beta,
        "w_pr": n(ks[10], (DEPTH, R_WIDTH, D_MODEL)) * R_WIDTH ** -0.5 * beta,
        "w_pa": n(ks[11], (DEPTH, A_WIDTH, D_MODEL)) * A_WIDTH ** -0.5 * beta,
        "w_o": n(ks[12], (DEPTH, D_MODEL, D_MODEL)) * D_MODEL ** -0.5 * beta,
        "ln1_g": 1.0 + 0.01 * n(ks[13], (DEPTH, D_MODEL)),
        "ln1_b": 0.01 * n(ks[14], (DEPTH, D_MODEL)),
        "w_gate": n(ks[15], (DEPTH, D_MODEL, D_FF)) * D_MODEL ** -0.5,
        "w_up": n(ks[16], (DEPTH, D_MODEL, D_FF)) * D_MODEL ** -0.5,
        "w_down": n(ks[17], (DEPTH, D_FF, D_MODEL)) * D_FF ** -0.5 * beta,
        "ln2_g": 1.0 + 0.01 * n(ks[18], (DEPTH, D_MODEL)),
        "ln2_b": 0.01 * n(ks[19], (DEPTH, D_MODEL)),
    }


def reference(x, w_in, gmlp_ln_g, gmlp_ln_b, gmlp_ws, gmlp_bs, hgrn_lb_logits, hgrn_norm_w,
              fox_fb, w_pg, w_pr, w_pa, w_o, ln1_g, ln1_b, w_gate, w_up, w_down, ln2_g, ln2_b):
    alpha = (2 * DEPTH) ** 0.25
    probs = jax.nn.softmax(hgrn_lb_logits.astype(jnp.float32), axis=0)
    lbs = jnp.cumsum(probs, axis=0) - probs[0:1]
    for l in range(DEPTH):
        y = mixer_block(x, w_in[l], gmlp_ln_g[l], gmlp_ln_b[l], gmlp_ws[l], gmlp_bs[l],
                        lbs[l], hgrn_norm_w[l], fox_fb[l], w_pg[l], w_pr[l], w_pa[l], w_o[l])
        x = layer_norm(alpha * x + y, ln1_g[l], ln1_b[l])
        x = layer_norm(alpha * x + swiglu(x, w_gate[l], w_up[l], w_down[l]), ln2_g[l], ln2_b[l])
    return x
```

```python
import functools

import numpy as np
import jax
import jax.numpy as jnp
from jax import lax
from jax.experimental import pallas as pl
from jax.experimental.pallas import tpu as pltpu

BF16 = jnp.bfloat16
F32 = jnp.float32

D_MODEL = 1024
G_GROUPS = 4
G_WIDTH = 512
G_CHUNK = 128
R_HEADS = 4
R_DK = 128
R_WIDTH = 512
A_HEADS = 8
A_HEAD_DIM = 64
A_WIDTH = 512
N_BRANCH = 3
D_FF = 2816
LN_EPS = 1e-5
RMS_EPS = 1e-6
NEG_BIG = -1e30
LB_FLOOR = 1e-30

LANES = 128
HGRN_CHUNK = 128
HGRN_LEVELS = 7
VMEM_LIMIT_BYTES = 56 * 1024 * 1024


def _dot(a, b):
    return jnp.dot(a, b, preferred_element_type=F32)


def _dot_nt(a, b):
    return lax.dot_general(a, b, (((1,), (1,)), ((), ())), preferred_element_type=F32)


def _sigmoid(z):
    return 1.0 / (1.0 + jnp.exp(-z))


def _log_sigmoid(z):
    return jnp.minimum(z, 0.0) - jnp.log1p(jnp.exp(-jnp.abs(z)))


def _gelu(z):
    return 0.5 * z * (1.0 + lax.erf(z * np.float32(2.0 ** -0.5)))


def _layer_norm(r, g, b):
    mu = jnp.mean(r, axis=-1, keepdims=True)
    d = r - mu
    var = jnp.mean(d * d, axis=-1, keepdims=True)
    return d * lax.rsqrt(var + LN_EPS) * g + b


def _split_bf16(a, terms):
    parts = []
    rem = a
    for _ in range(terms):
        p = rem.astype(BF16)
        parts.append(p)
        rem = rem - p.astype(F32)
    return parts


def _params(sem):
    return pltpu.CompilerParams(dimension_semantics=sem, vmem_limit_bytes=VMEM_LIMIT_BYTES)


def _gmlp_kernel(x_ref, w_ref, lng_ref, lnb_ref, ws_ref, bs_ref, o_ref):
    tm = x_ref.shape[0]
    h = _dot(x_ref[...].astype(BF16), w_ref[...])
    u = _gelu(h[:, :G_WIDTH])
    v = _layer_norm(_gelu(h[:, G_WIDTH:]), lng_ref[...], lnb_ref[...]).astype(BF16)
    row = lax.broadcasted_iota(jnp.int32, (G_CHUNK, G_CHUNK), 0)
    col = lax.broadcasted_iota(jnp.int32, (G_CHUNK, G_CHUNK), 1)
    for g in range(G_GROUPS):
        w_causal = jnp.where(row >= col, ws_ref[g], 0.0).astype(BF16)
        cs = slice(g * LANES, (g + 1) * LANES)
        for n in range(tm // G_CHUNK):
            rs = slice(n * G_CHUNK, (n + 1) * G_CHUNK)
            sv = _dot(w_causal, v[rs, cs]) + bs_ref[g]
            o_ref[rs, cs] = (u[rs, cs] * sv).astype(BF16)


def _gmlp_call(x2, w_uv, layer, ln_g, ln_b, ws, bs_b, tm):
    T = x2.shape[0]
    return pl.pallas_call(
        _gmlp_kernel,
        grid=(T // tm,),
        in_specs=[
            pl.BlockSpec((tm, D_MODEL), lambda i: (i, 0)),
            pl.BlockSpec((None, D_MODEL, 2 * G_WIDTH), lambda i: (layer, 0, 0)),
            pl.BlockSpec((1, G_WIDTH), lambda i: (0, 0)),
            pl.BlockSpec((1, G_WIDTH), lambda i: (0, 0)),
            pl.BlockSpec((G_GROUPS, G_CHUNK, G_CHUNK), lambda i: (0, 0, 0)),
            pl.BlockSpec((G_GROUPS, G_CHUNK, LANES), lambda i: (0, 0, 0)),
        ],
        out_specs=pl.BlockSpec((tm, G_WIDTH), lambda i: (i, 0)),
        out_shape=jax.ShapeDtypeStruct((T, G_WIDTH), BF16),
        compiler_params=_params(("parallel",)),
        name="gmlp",
    )(x2, w_uv, ln_g, ln_b, ws, bs_b)


def _hgrn_constants():
    c = HGRN_CHUNK
    t = np.arange(c)[:, None]
    tau = np.arange(c)[None, :]
    blocks = []
    for level in range(HGRN_LEVELS):
        width = 1 << level
        start = (t >> level) * width
        end = start + width - 1
        odd = ((t >> level) & 1) == 1
        blocks.append(np.where(odd, (tau >= start) & (tau <= t), (tau > t) & (tau <= end)))
    blocks.append(tau <= t)
    blocks.append(tau > t)
    selectors = np.concatenate(blocks, axis=0).astype(np.float32)
    xor = np.maximum(t ^ tau, 1)
    level_map = np.where(tau < t, np.floor(np.log2(xor)).astype(np.int32),
                         np.where(tau == t, -1, -2)).astype(np.int32)
    return selectors, level_map


def _hgrn_kernel(x_ref, w_ref, lb_ref, nw_ref, sel_ref, lev_ref, o_ref, proj_ref, state_ref):
    rows = x_ref.shape[0]
    c = HGRN_CHUNK

    @pl.when(pl.program_id(1) == 0)
    def _():
        state_ref[...] = jnp.zeros_like(state_ref)

    proj_ref[...] = _dot(x_ref[...].astype(BF16), w_ref[...])
    lb = lb_ref[...]
    log_lb = jnp.log(jnp.maximum(lb, LB_FLOOR))
    log_1m_lb = jnp.log1p(-lb)
    one_m_lb = 1.0 - lb
    norm_w = nw_ref[...]

    def chunk(ci, carry):
        r0 = pl.multiple_of(ci * c, c)
        rs = pl.ds(r0, c)
        z = proj_ref[rs, R_WIDTH:2 * R_WIDTH]
        b = log_1m_lb + _log_sigmoid(z)
        lf = jnp.maximum(log_lb, b) + jnp.log1p(jnp.exp(-jnp.abs(log_lb - b)))
        key = one_m_lb * _sigmoid(-z)
        lf_hi, lf_lo = _split_bf16(lf, 2)
        sel = sel_ref[...]
        decay = jnp.exp(_dot(sel, lf_hi) + _dot(sel, lf_lo))
        lev = lev_ref[...]
        for h in range(R_HEADS):
            hs = slice(h * LANES, (h + 1) * LANES)
            q = proj_ref[rs, h * LANES:(h + 1) * LANES]
            kh = key[:, hs]
            val = proj_ref[rs, 2 * R_WIDTH + h * LANES:2 * R_WIDTH + (h + 1) * LANES]
            gate = proj_ref[rs, 3 * R_WIDTH + h * LANES:3 * R_WIDTH + (h + 1) * LANES]
            scores = jnp.zeros((c, c), F32)
            for level in range(HGRN_LEVELS):
                e = decay[level * c:(level + 1) * c, hs]
                s_l = _dot_nt((q * e).astype(BF16), (kh * e).astype(BF16))
                scores = jnp.where(lev == level, s_l, scores)
            diag = jnp.sum(q * kh, axis=-1, keepdims=True)
            scores = jnp.where(lev == -1, diag, scores)
            val_b = val.astype(BF16)
            intra = _dot(scores.astype(BF16), val_b)
            e_in = decay[HGRN_LEVELS * c:(HGRN_LEVELS + 1) * c, hs]
            e_out = decay[(HGRN_LEVELS + 1) * c:(HGRN_LEVELS + 2) * c, hs]
            state = state_ref[h]
            inter = _dot_nt((q * e_in).astype(BF16), state.astype(BF16))
            e_last = e_in[c - 1:c, :]
            state_ref[h] = state * e_last + _dot(val.T.astype(BF16), (kh * e_out).astype(BF16))
            o = inter + intra
            o = o * lax.rsqrt(jnp.mean(o * o, axis=-1, keepdims=True) + RMS_EPS)
            o = o * norm_w * (gate * _sigmoid(gate))
            o_ref[rs, h * LANES:(h + 1) * LANES] = o.astype(BF16)
        return carry

    lax.fori_loop(0, rows // c, chunk, 0)


def _hgrn_call(x2, w_r, layer, lb, norm_w, batch, seq, rows):
    T = x2.shape[0]
    nr = seq // rows
    selectors, level_map = _hgrn_constants()
    n_sel = selectors.shape[0]
    return pl.pallas_call(
        _hgrn_kernel,
        grid=(batch, nr),
        in_specs=[
            pl.BlockSpec((rows, D_MODEL), lambda b, r: (b * nr + r, 0)),
            pl.BlockSpec((None, D_MODEL, 4 * R_WIDTH), lambda b, r: (layer, 0, 0)),
            pl.BlockSpec((1, R_WIDTH), lambda b, r: (0, 0)),
            pl.BlockSpec((1, R_DK), lambda b, r: (0, 0)),
            pl.BlockSpec((n_sel, HGRN_CHUNK), lambda b, r: (0, 0)),
            pl.BlockSpec((HGRN_CHUNK, HGRN_CHUNK), lambda b, r: (0, 0)),
        ],
        out_specs=pl.BlockSpec((rows, R_WIDTH), lambda b, r: (b * nr + r, 0)),
        out_shape=jax.ShapeDtypeStruct((T, R_WIDTH), BF16),
        scratch_shapes=[
            pltpu.VMEM((rows, 4 * R_WIDTH), F32),
            pltpu.VMEM((R_HEADS, R_DK, R_DK), F32),
        ],
        compiler_params=_params(("arbitrary", "arbitrary")),
        name="hgrn",
    )(x2, w_r, lb, norm_w, jnp.asarray(selectors, BF16), jnp.asarray(level_map))


def _fox_proj_kernel(x_ref, wqkv_ref, wf_ref, fb_ref, tri_ref, q_ref, k_ref, v_ref, f_ref, carry_ref):
    rows = x_ref.shape[0]

    @pl.when(pl.program_id(1) == 0)
    def _():
        carry_ref[...] = jnp.zeros_like(carry_ref)

    xb = x_ref[...].astype(BF16)
    qkv = _dot(xb, wqkv_ref[...])
    q_ref[...] = (qkv[:, :A_WIDTH] * np.float32(A_HEAD_DIM ** -0.5)).astype(BF16)
    k_ref[...] = qkv[:, A_WIDTH:2 * A_WIDTH].astype(BF16)
    v_ref[...] = qkv[:, 2 * A_WIDTH:].astype(BF16)
    log_f = _log_sigmoid(_dot(xb, wf_ref[...]) + fb_ref[...])
    tri = tri_ref[...]
    p1, p2, p3 = _split_bf16(log_f, 3)
    cum = (_dot(tri, p3) + _dot(tri, p2)) + _dot(tri, p1) + carry_ref[...]
    f_ref[...] = cum
    carry_ref[...] = cum[rows - 1:rows, :]


def _fox_proj_call(x2, w_qkv, w_f, layer, fb, batch, seq, rows):
    T = x2.shape[0]
    nr = seq // rows
    tri = jnp.asarray(np.tril(np.ones((rows, rows), np.float32)), BF16)
    act = jax.ShapeDtypeStruct((T, A_WIDTH), BF16)
    row_block = lambda width: pl.BlockSpec((rows, width), lambda b, r: (b * nr + r, 0))
    return pl.pallas_call(
        _fox_proj_kernel,
        grid=(batch, nr),
        in_specs=[
            row_block(D_MODEL),
            pl.BlockSpec((None, D_MODEL, 3 * A_WIDTH), lambda b, r: (layer, 0, 0)),
            pl.BlockSpec((None, D_MODEL, LANES), lambda b, r: (layer, 0, 0)),
            pl.BlockSpec((1, LANES), lambda b, r: (0, 0)),
            pl.BlockSpec((rows, rows), lambda b, r: (0, 0)),
        ],
        out_specs=[row_block(A_WIDTH), row_block(A_WIDTH), row_block(A_WIDTH), row_block(LANES)],
        out_shape=[act, act, act, jax.ShapeDtypeStruct((T, LANES), F32)],
        scratch_shapes=[pltpu.VMEM((1, LANES), F32)],
        compiler_params=_params(("arbitrary", "arbitrary")),
        name="fox_proj",
    )(x2, w_qkv, w_f, fb, tri)


def _fox_attn_kernel(q_ref, k_ref, v_ref, fk_ref, o_ref):
    tq = q_ref.shape[0]
    i = pl.program_id(2)
    q = q_ref[...]
    lane = lax.broadcasted_iota(jnp.int32, (tq, LANES), 1)
    row = lax.broadcasted_iota(jnp.int32, (tq, tq), 0)
    col = lax.broadcasted_iota(jnp.int32, (tq, tq), 1)
    outs = []
    for h in range(2):
        in_head = (lane >= A_HEAD_DIM) if h else (lane < A_HEAD_DIM)
        qh = jnp.where(in_head, q, jnp.zeros_like(q))

        def step(j, carry, masked, h=h, qh=qh):
            m, l, acc = carry
            c0 = pl.multiple_of(j * tq, tq)
            kj = k_ref[pl.ds(c0, tq), :]
            vj = v_ref[pl.ds(c0, tq), :]
            x = _dot_nt(qh, kj) - fk_ref[h:h + 1, pl.ds(c0, tq)]
            if masked:
                x = jnp.where(col <= row, x, NEG_BIG)
            m_new = jnp.maximum(m, jnp.max(x, axis=-1, keepdims=True))
            alpha = jnp.exp(m - m_new)
            p = jnp.exp(x - m_new)
            l = alpha * l + jnp.sum(p, axis=-1, keepdims=True)
            acc = alpha * acc + _dot(p.astype(BF16), vj)
            return m_new, l, acc

        init = (jnp.full((tq, 1), NEG_BIG, F32), jnp.zeros((tq, 1), F32), jnp.zeros((tq, LANES), F32))
        carry = lax.fori_loop(0, i, functools.partial(step, masked=False), init)
        _, l, acc = step(i, carry, masked=True)
        outs.append(acc / l)
    o_ref[...] = jnp.where(lane < A_HEAD_DIM, outs[0], outs[1]).astype(BF16)


def _fox_attn_call(q, k, v, fk, batch, seq, tq):
    T = q.shape[0]
    nq = seq // tq
    pairs = A_HEADS // 2
    return pl.pallas_call(
        _fox_attn_kernel,
        grid=(batch, pairs, nq),
        in_specs=[
            pl.BlockSpec((tq, LANES), lambda b, p, i: (b * nq + i, p)),
            pl.BlockSpec((seq, LANES), lambda b, p, i: (b, p)),
            pl.BlockSpec((seq, LANES), lambda b, p, i: (b, p)),
            pl.BlockSpec((None, None, 2, seq), lambda b, p, i: (b, p, 0, 0)),
        ],
        out_specs=pl.BlockSpec((tq, LANES), lambda b, p, i: (b * nq + i, p)),
        out_shape=jax.ShapeDtypeStruct((T, A_WIDTH), BF16),
        compiler_params=_params(("parallel", "parallel", "arbitrary")),
        name="fox_attn",
    )(q, k, v, fk)


def _merge_kernel(x_ref, yg_ref, yr_ref, ya_ref, wgate_ref, wpg_ref, wpr_ref, wpa_ref, wo_ref,
                  g_ref, b_ref, o_ref, *, alpha):
    x = x_ref[...]
    xb = x.astype(BF16)
    mixed = None
    for n, (y_ref, wp_ref) in enumerate(((yg_ref, wpg_ref), (yr_ref, wpr_ref), (ya_ref, wpa_ref))):
        gate = _sigmoid(_dot(xb, wgate_ref[:, n * D_MODEL:(n + 1) * D_MODEL]))
        term = gate * _dot(y_ref[...], wp_ref[...])
        mixed = term if mixed is None else mixed + term
    out = _dot(mixed.astype(BF16), wo_ref[...])
    o_ref[...] = _layer_norm(alpha * x + out, g_ref[...], b_ref[...])


def _merge_call(x2, yg, yr, ya, w_gate, w_pg, w_pr, w_pa, w_o, layer, ln_g, ln_b, alpha, tm):
    T = x2.shape[0]
    rows = lambda width: pl.BlockSpec((tm, width), lambda i: (i, 0))
    weight = lambda r, c: pl.BlockSpec((None, r, c), lambda i: (layer, 0, 0))
    vec = pl.BlockSpec((1, D_MODEL), lambda i: (0, 0))
    return pl.pallas_call(
        functools.partial(_merge_kernel, alpha=alpha),
        grid=(T // tm,),
        in_specs=[
            rows(D_MODEL), rows(G_WIDTH), rows(R_WIDTH), rows(A_WIDTH),
            weight(D_MODEL, N_BRANCH * D_MODEL),
            weight(G_WIDTH, D_MODEL), weight(R_WIDTH, D_MODEL), weight(A_WIDTH, D_MODEL),
            weight(D_MODEL, D_MODEL), vec, vec,
        ],
        out_specs=rows(D_MODEL),
        out_shape=jax.ShapeDtypeStruct((T, D_MODEL), F32),
        compiler_params=_params(("parallel",)),
        name="merge",
    )(x2, yg, yr, ya, w_gate, w_pg, w_pr, w_pa, w_o, ln_g, ln_b)


def _ffn_kernel(x_ref, wg_ref, wu_ref, wd_ref, g_ref, b_ref, o_ref, *, alpha, ff_chunk):
    x = x_ref[...]
    xb = x.astype(BF16)
    out = None
    for c in range(D_FF // ff_chunk):
        cs = slice(c * ff_chunk, (c + 1) * ff_chunk)
        hg = _dot(xb, wg_ref[:, cs])
        hu = _dot(xb, wu_ref[:, cs])
        hid = (hg * _sigmoid(hg) * hu).astype(BF16)
        part = _dot(hid, wd_ref[cs, :])
        out = part if out is None else out + part
    o_ref[...] = _layer_norm(alpha * x + out, g_ref[...], b_ref[...])


def _ffn_call(x2, w_gate, w_up, w_down, layer, ln_g, ln_b, alpha, tm, ff_chunk):
    T = x2.shape[0]
    rows = pl.BlockSpec((tm, D_MODEL), lambda i: (i, 0))
    vec = pl.BlockSpec((1, D_MODEL), lambda i: (0, 0))
    return pl.pallas_call(
        functools.partial(_ffn_kernel, alpha=alpha, ff_chunk=ff_chunk),
        grid=(T // tm,),
        in_specs=[
            rows,
            pl.BlockSpec((None, D_MODEL, D_FF), lambda i: (layer, 0, 0)),
            pl.BlockSpec((None, D_MODEL, D_FF), lambda i: (layer, 0, 0)),
            pl.BlockSpec((None, D_FF, D_MODEL), lambda i: (layer, 0, 0)),
            vec, vec,
        ],
        out_specs=rows,
        out_shape=jax.ShapeDtypeStruct((T, D_MODEL), F32),
        compiler_params=_params(("parallel",)),
        name="ffn",
    )(x2, w_gate, w_up, w_down, ln_g, ln_b)


def kernel(x, w_in, gmlp_ln_g, gmlp_ln_b, gmlp_ws, gmlp_bs, hgrn_lb_logits, hgrn_norm_w, fox_fb,
           w_pg, w_pr, w_pa, w_o, ln1_g, ln1_b, w_gate, w_up, w_down, ln2_g, ln2_b):
    batch, seq, _ = x.shape
    depth = w_in.shape[0]
    T = batch * seq
    alpha = float((2 * depth) ** 0.25)
    rows = min(512, seq)
    tq = min(256, seq)
    tm = 256

    o_r = 2 * G_WIDTH
    o_a = o_r + 4 * R_WIDTH
    o_f = o_a + 3 * A_WIDTH
    o_gate = o_f + A_HEADS
    w_uv = w_in[:, :, :o_r].astype(BF16)
    w_r = w_in[:, :, o_r:o_a].astype(BF16)
    w_qkv = w_in[:, :, o_a:o_f].astype(BF16)
    w_f = jnp.pad(w_in[:, :, o_f:o_gate], ((0, 0), (0, 0), (0, LANES - A_HEADS))).astype(BF16)
    w_bgate = w_in[:, :, o_gate:].astype(BF16)
    w_pg_b, w_pr_b, w_pa_b, w_o_b = (w.astype(BF16) for w in (w_pg, w_pr, w_pa, w_o))
    w_gate_b, w_up_b, w_down_b = (w.astype(BF16) for w in (w_gate, w_up, w_down))

    probs = jax.nn.softmax(hgrn_lb_logits.astype(F32), axis=0)
    lbs = jnp.cumsum(probs, axis=0) - probs[0:1]
    fb_pad = jnp.pad(fox_fb.astype(F32), ((0, 0), (0, LANES - A_HEADS)))

    x2 = x.reshape(T, D_MODEL)
    for l in range(depth):
        bs_b = jnp.broadcast_to(gmlp_bs[l][:, :, None], (G_GROUPS, G_CHUNK, LANES))
        y_g = _gmlp_call(x2, w_uv, l, gmlp_ln_g[l][None], gmlp_ln_b[l][None], gmlp_ws[l], bs_b, tm)
        y_r = _hgrn_call(x2, w_r, l, lbs[l][None], hgrn_norm_w[l][None], batch, seq, rows)
        q, k, v, f_cum = _fox_proj_call(x2, w_qkv, w_f, l, fb_pad[l][None], batch, seq, rows)
        fk = f_cum[:, :A_HEADS].reshape(batch, seq, A_HEADS).transpose(0, 2, 1)
        fk = fk.reshape(batch, A_HEADS // 2, 2, seq)
        y_a = _fox_attn_call(q, k, v, fk, batch, seq, tq)
        x2 = _merge_call(x2, y_g, y_r, y_a, w_bgate, w_pg_b, w_pr_b, w_pa_b, w_o_b, l,
                         ln1_g[l][None], ln1_b[l][None], alpha, tm)
        x2 = _ffn_call(x2, w_gate_b, w_up_b, w_down_b, l, ln2_g[l][None], ln2_b[l][None], alpha,
                       tm, D_FF // 2)
    return x2.reshape(batch, seq, D_MODEL)
```

```python
import functools

import numpy as np
import jax
import jax.numpy as jnp
from jax import lax
from jax.experimental import pallas as pl
from jax.experimental.pallas import tpu as pltpu

BF16 = jnp.bfloat16
F32 = jnp.float32

D_MODEL = 1024
G_GROUPS = 4
G_WIDTH = 512
G_CHUNK = 128
R_HEADS = 4
R_DK = 128
R_WIDTH = 512
A_HEADS = 8
A_HEAD_DIM = 64
A_WIDTH = 512
N_BRANCH = 3
D_FF = 2816
LN_EPS = 1e-5
RMS_EPS = 1e-6
NEG_BIG = -1e30
LB_FLOOR = 1e-30

LANES = 128
HGRN_CHUNK = 128
HGRN_LEVELS = 7
VMEM_LIMIT_BYTES = 56 * 1024 * 1024


def _dot(a, b):
    return jnp.dot(a, b, preferred_element_type=F32)


def _dot_nt(a, b):
    return lax.dot_general(a, b, (((1,), (1,)), ((), ())), preferred_element_type=F32)


def _sigmoid(z):
    return 1.0 / (1.0 + jnp.exp(-z))


def _log_sigmoid(z):
    return jnp.minimum(z, 0.0) - jnp.log1p(jnp.exp(-jnp.abs(z)))


def _gelu(z):
    return 0.5 * z * (1.0 + lax.erf(z * np.float32(2.0 ** -0.5)))


def _layer_norm(r, g, b):
    mu = jnp.mean(r, axis=-1, keepdims=True)
    d = r - mu
    var = jnp.mean(d * d, axis=-1, keepdims=True)
    return d * lax.rsqrt(var + LN_EPS) * g + b


def _split_bf16(a, terms):
    parts = []
    rem = a
    for _ in range(terms):
        p = rem.astype(BF16)
        parts.append(p)
        rem = rem - p.astype(F32)
    return parts


def _params(sem):
    return pltpu.CompilerParams(dimension_semantics=sem, vmem_limit_bytes=VMEM_LIMIT_BYTES)


def _gmlp_kernel(x_ref, w_ref, lng_ref, lnb_ref, ws_ref, bs_ref, o_ref):
    tm = x_ref.shape[0]
    h = _dot(x_ref[...].astype(BF16), w_ref[...])
    u = _gelu(h[:, :G_WIDTH])
    v = _layer_norm(_gelu(h[:, G_WIDTH:]), lng_ref[...], lnb_ref[...]).astype(BF16)
    row = lax.broadcasted_iota(jnp.int32, (G_CHUNK, G_CHUNK), 0)
    col = lax.broadcasted_iota(jnp.int32, (G_CHUNK, G_CHUNK), 1)
    for g in range(G_GROUPS):
        w_causal = jnp.where(row >= col, ws_ref[g], 0.0).astype(BF16)
        cs = slice(g * LANES, (g + 1) * LANES)
        for n in range(tm // G_CHUNK):
            rs = slice(n * G_CHUNK, (n + 1) * G_CHUNK)
            sv = _dot(w_causal, v[rs, cs]) + bs_ref[g]
            o_ref[rs, cs] = (u[rs, cs] * sv).astype(BF16)


def _gmlp_call(x2, w_uv, layer, ln_g, ln_b, ws, bs_b, tm):
    T = x2.shape[0]
    return pl.pallas_call(
        _gmlp_kernel,
        grid=(T // tm,),
        in_specs=[
            pl.BlockSpec((tm, D_MODEL), lambda i: (i, 0)),
            pl.BlockSpec((None, D_MODEL, 2 * G_WIDTH), lambda i: (layer, 0, 0)),
            pl.BlockSpec((1, G_WIDTH), lambda i: (0, 0)),
            pl.BlockSpec((1, G_WIDTH), lambda i: (0, 0)),
            pl.BlockSpec((G_GROUPS, G_CHUNK, G_CHUNK), lambda i: (0, 0, 0)),
            pl.BlockSpec((G_GROUPS, G_CHUNK, LANES), lambda i: (0, 0, 0)),
        ],
        out_specs=pl.BlockSpec((tm, G_WIDTH), lambda i: (i, 0)),
        out_shape=jax.ShapeDtypeStruct((T, G_WIDTH), BF16),
        compiler_params=_params(("parallel",)),
        name="gmlp",
    )(x2, w_uv, ln_g, ln_b, ws, bs_b)


def _hgrn_constants():
    c = HGRN_CHUNK
    t = np.arange(c)[:, None]
    tau = np.arange(c)[None, :]
    blocks = []
    for level in range(HGRN_LEVELS):
        width = 1 << level
        start = (t >> level) * width
        end = start + width - 1
        odd = ((t >> level) & 1) == 1
        blocks.append(np.where(odd, (tau >= start) & (tau <= t), (tau > t) & (tau <= end)))
    blocks.append(tau <= t)
    blocks.append(tau > t)
    selectors = np.concatenate(blocks, axis=0).astype(np.float32)
    xor = np.maximum(t ^ tau, 1)
    level_map = np.where(tau < t, np.floor(np.log2(xor)).astype(np.int32),
                         np.where(tau == t, -1, -2)).astype(np.int32)
    return selectors, level_map


def _hgrn_kernel(x_ref, w_ref, lb_ref, nw_ref, sel_ref, lev_ref, o_ref, proj_ref, state_ref):
    rows = x_ref.shape[0]
    c = HGRN_CHUNK

    @pl.when(pl.program_id(1) == 0)
    def _():
        state_ref[...] = jnp.zeros_like(state_ref)

    proj_ref[...] = _dot(x_ref[...].astype(BF16), w_ref[...])
    lb = lb_ref[...]
    log_lb = jnp.log(jnp.maximum(lb, LB_FLOOR))
    log_1m_lb = jnp.log1p(-lb)
    one_m_lb = 1.0 - lb
    norm_w = nw_ref[...]

    def chunk(ci, carry):
        r0 = pl.multiple_of(ci * c, c)
        rs = pl.ds(r0, c)
        z = proj_ref[rs, R_WIDTH:2 * R_WIDTH]
        b = log_1m_lb + _log_sigmoid(z)
        lf = jnp.maximum(log_lb, b) + jnp.log1p(jnp.exp(-jnp.abs(log_lb - b)))
        key = one_m_lb * _sigmoid(-z)
        lf_hi, lf_lo = _split_bf16(lf, 2)
        sel = sel_ref[...]
        decay = jnp.exp(_dot(sel, lf_hi) + _dot(sel, lf_lo))
        lev = lev_ref[...]
        for h in range(R_HEADS):
            hs = slice(h * LANES, (h + 1) * LANES)
            q = proj_ref[rs, h * LANES:(h + 1) * LANES]
            kh = key[:, hs]
            val = proj_ref[rs, 2 * R_WIDTH + h * LANES:2 * R_WIDTH + (h + 1) * LANES]
            gate = proj_ref[rs, 3 * R_WIDTH + h * LANES:3 * R_WIDTH + (h + 1) * LANES]
            scores = jnp.zeros((c, c), F32)
            for level in range(HGRN_LEVELS):
                e = decay[level * c:(level + 1) * c, hs]
                s_l = _dot_nt((q * e).astype(BF16), (kh * e).astype(BF16))
                scores = jnp.where(lev == level, s_l, scores)
            diag = jnp.sum(q * kh, axis=-1, keepdims=True)
            scores = jnp.where(lev == -1, diag, scores)
            val_b = val.astype(BF16)
            intra = _dot(scores.astype(BF16), val_b)
            e_in = decay[HGRN_LEVELS * c:(HGRN_LEVELS + 1) * c, hs]
            e_out = decay[(HGRN_LEVELS + 1) * c:(HGRN_LEVELS + 2) * c, hs]
            state = state_ref[h]
            inter = _dot_nt((q * e_in).astype(BF16), state.astype(BF16))
            e_last = e_in[c - 1:c, :]
            state_ref[h] = state * e_last + _dot(val.T.astype(BF16), (kh * e_out).astype(BF16))
            o = inter + intra
            o = o * lax.rsqrt(jnp.mean(o * o, axis=-1, keepdims=True) + RMS_EPS)
            o = o * norm_w * (gate * _sigmoid(gate))
            o_ref[rs, h * LANES:(h + 1) * LANES] = o.astype(BF16)
        return carry

    lax.fori_loop(0, rows // c, chunk, 0)


def _hgrn_call(x2, w_r, layer, lb, norm_w, batch, seq, rows):
    T = x2.shape[0]
    nr = seq // rows
    selectors, level_map = _hgrn_constants()
    n_sel = selectors.shape[0]
    return pl.pallas_call(
        _hgrn_kernel,
        grid=(batch, nr),
        in_specs=[
            pl.BlockSpec((rows, D_MODEL), lambda b, r: (b * nr + r, 0)),
            pl.BlockSpec((None, D_MODEL, 4 * R_WIDTH), lambda b, r: (layer, 0, 0)),
            pl.BlockSpec((1, R_WIDTH), lambda b, r: (0, 0)),
            pl.BlockSpec((1, R_DK), lambda b, r: (0, 0)),
            pl.BlockSpec((n_sel, HGRN_CHUNK), lambda b, r: (0, 0)),
            pl.BlockSpec((HGRN_CHUNK, HGRN_CHUNK), lambda b, r: (0, 0)),
        ],
        out_specs=pl.BlockSpec((rows, R_WIDTH), lambda b, r: (b * nr + r, 0)),
        out_shape=jax.ShapeDtypeStruct((T, R_WIDTH), BF16),
        scratch_shapes=[
            pltpu.VMEM((rows, 4 * R_WIDTH), F32),
            pltpu.VMEM((R_HEADS, R_DK, R_DK), F32),
        ],
        compiler_params=_params(("arbitrary", "arbitrary")),
        name="hgrn",
    )(x2, w_r, lb, norm_w, jnp.asarray(selectors, BF16), jnp.asarray(level_map))


FOX_F_TERMS = 3
LOG2E = float(np.log2(np.e))


def _fox_placement():
    place = np.zeros((FOX_F_TERMS * LANES, 2 * A_WIDTH), np.float32)
    for h in range(A_HEADS):
        pair, odd = divmod(h, 2)
        spare0 = pair * LANES + (0 if odd else A_HEAD_DIM)
        for t in range(FOX_F_TERMS):
            place[t * LANES + h, odd * A_WIDTH + spare0 + t] = 1.0
    return place


def _fox_proj_kernel(x_ref, wqkv_ref, wf_ref, fb_ref, tri_ref, place_ref,
                     q_ref, k0_ref, k1_ref, v0_ref, v1_ref, carry_ref):
    rows = x_ref.shape[0]

    @pl.when(pl.program_id(1) == 0)
    def _():
        carry_ref[...] = jnp.zeros_like(carry_ref)

    xb = x_ref[...].astype(BF16)
    qkv = _dot(xb, wqkv_ref[...])
    q_ref[...] = (qkv[:, :A_WIDTH] * np.float32(A_HEAD_DIM ** -0.5 * LOG2E)).astype(BF16)
    log_f = _log_sigmoid(_dot(xb, wf_ref[...]) + fb_ref[...])
    tri = tri_ref[...]
    p1, p2, p3 = _split_bf16(log_f, 3)
    cum = (_dot(tri, p3) + _dot(tri, p2)) + _dot(tri, p1) + carry_ref[...]
    carry_ref[...] = cum[rows - 1:rows, :]
    pieces = jnp.concatenate(_split_bf16(cum * np.float32(-LOG2E), FOX_F_TERMS), axis=1)
    placed = _dot(pieces, place_ref[...])
    even_half = (lax.broadcasted_iota(jnp.int32, (rows, A_WIDTH), 1) & A_HEAD_DIM) == 0
    k = qkv[:, A_WIDTH:2 * A_WIDTH]
    v = qkv[:, 2 * A_WIDTH:]
    k0_ref[...] = jnp.where(even_half, k, placed[:, :A_WIDTH]).astype(BF16)
    k1_ref[...] = jnp.where(even_half, placed[:, A_WIDTH:], k).astype(BF16)
    v0_ref[...] = jnp.where(even_half, v, 1.0).astype(BF16)
    v1_ref[...] = jnp.where(even_half, 1.0, v).astype(BF16)


def _fox_proj_call(x2, w_qkv, w_f, layer, fb, batch, seq, rows):
    T = x2.shape[0]
    nr = seq // rows
    tri = jnp.asarray(np.tril(np.ones((rows, rows), np.float32)), BF16)
    place = jnp.asarray(_fox_placement(), BF16)
    act = jax.ShapeDtypeStruct((T, A_WIDTH), BF16)
    row_block = lambda width: pl.BlockSpec((rows, width), lambda b, r: (b * nr + r, 0))
    return pl.pallas_call(
        _fox_proj_kernel,
        grid=(batch, nr),
        in_specs=[
            row_block(D_MODEL),
            pl.BlockSpec((None, D_MODEL, 3 * A_WIDTH), lambda b, r: (layer, 0, 0)),
            pl.BlockSpec((None, D_MODEL, LANES), lambda b, r: (layer, 0, 0)),
            pl.BlockSpec((1, LANES), lambda b, r: (0, 0)),
            pl.BlockSpec((rows, rows), lambda b, r: (0, 0)),
            pl.BlockSpec(place.shape, lambda b, r: (0, 0)),
        ],
        out_specs=[row_block(A_WIDTH)] * 5,
        out_shape=[act] * 5,
        scratch_shapes=[pltpu.VMEM((1, LANES), F32)],
        compiler_params=_params(("arbitrary", "arbitrary")),
        name="fox_proj",
    )(x2, w_qkv, w_f, fb, tri, place)


def _fox_attn_kernel(q_ref, k0_ref, k1_ref, v0_ref, v1_ref, o_ref, x_ref, m_ref, acc_ref):
    tq = q_ref.shape[0]
    i = pl.program_id(2)
    q = q_ref[...].astype(F32)
    lane = lax.broadcasted_iota(jnp.int32, (tq, LANES), 1)
    row = lax.broadcasted_iota(jnp.int32, (tq, tq), 0)
    col = lax.broadcasted_iota(jnp.int32, (tq, tq), 1)
    low_half = lane < A_HEAD_DIM
    q_aug = (jnp.where(low_half, q, jnp.where(lane < A_HEAD_DIM + FOX_F_TERMS, 1.0, 0.0)).astype(BF16),
             jnp.where(low_half, jnp.where(lane < FOX_F_TERMS, 1.0, 0.0), q).astype(BF16))
    k_refs = (k0_ref, k1_ref)
    v_refs = (v0_ref, v1_ref)

    def scores(j, slot):
        c0 = pl.multiple_of(j * tq, tq)
        for h in range(2):
            x_ref[slot, h] = _dot_nt(q_aug[h], k_refs[h][pl.ds(c0, tq), :])

    def accumulate(j, slot, masked):
        c0 = pl.multiple_of(j * tq, tq)
        for h in range(2):
            m = m_ref[h]
            x = x_ref[slot, h]
            if masked:
                x = jnp.where(col <= row, x, NEG_BIG)
            m_new = jnp.maximum(m, jnp.max(x, axis=-1, keepdims=True))
            p = jnp.exp2(x - m_new).astype(BF16)
            acc_ref[h] = jnp.exp2(m - m_new) * acc_ref[h] + _dot(p, v_refs[h][pl.ds(c0, tq), :])
            m_ref[h] = m_new

    m_ref[...] = jnp.full(m_ref.shape, NEG_BIG, F32)
    acc_ref[...] = jnp.zeros(acc_ref.shape, F32)
    scores(0, 0)

    def pair(jj, carry):
        j = 2 * jj
        scores(j + 1, 1)
        accumulate(j, 0, masked=False)
        scores(j + 2, 0)
        accumulate(j + 1, 1, masked=False)
        return carry

    lax.fori_loop(0, i // 2, pair, 0)

    @pl.when(i % 2 == 1)
    def _():
        scores(i, 1)
        accumulate(i - 1, 0, masked=False)
        accumulate(i, 1, masked=True)

    @pl.when(i % 2 == 0)
    def _():
        accumulate(i, 0, masked=True)

    acc0 = acc_ref[0]
    acc1 = acc_ref[1]
    out0 = acc0 / pltpu.roll(acc0, A_HEAD_DIM, axis=1)
    out1 = acc1 / pltpu.roll(acc1, A_HEAD_DIM, axis=1)
    o_ref[...] = jnp.where(low_half, out0, out1).astype(BF16)


def _fox_attn_call(q, k0, k1, v0, v1, batch, seq, tq):
    T = q.shape[0]
    nq = seq // tq
    pairs = A_HEADS // 2
    q_block = pl.BlockSpec((tq, LANES), lambda b, p, i: (b * nq + i, p))
    kv_block = pl.BlockSpec((seq, LANES), lambda b, p, i: (b, p))
    return pl.pallas_call(
        _fox_attn_kernel,
        grid=(batch, pairs, nq),
        in_specs=[q_block, kv_block, kv_block, kv_block, kv_block],
        out_specs=q_block,
        out_shape=jax.ShapeDtypeStruct((T, A_WIDTH), BF16),
        scratch_shapes=[
            pltpu.VMEM((2, 2, tq, tq), F32),
            pltpu.VMEM((2, tq, 1), F32),
            pltpu.VMEM((2, tq, LANES), F32),
        ],
        compiler_params=_params(("parallel", "parallel", "arbitrary")),
        name="fox_attn",
    )(q, k0, k1, v0, v1)


def _merge_kernel(x_ref, yg_ref, yr_ref, ya_ref, wgate_ref, wpg_ref, wpr_ref, wpa_ref, wo_ref,
                  g_ref, b_ref, o_ref, *, alpha):
    x = x_ref[...]
    xb = x.astype(BF16)
    mixed = None
    for n, (y_ref, wp_ref) in enumerate(((yg_ref, wpg_ref), (yr_ref, wpr_ref), (ya_ref, wpa_ref))):
        gate = _sigmoid(_dot(xb, wgate_ref[:, n * D_MODEL:(n + 1) * D_MODEL]))
        term = gate * _dot(y_ref[...], wp_ref[...])
        mixed = term if mixed is None else mixed + term
    out = _dot(mixed.astype(BF16), wo_ref[...])
    o_ref[...] = _layer_norm(alpha * x + out, g_ref[...], b_ref[...])


def _merge_call(x2, yg, yr, ya, w_gate, w_pg, w_pr, w_pa, w_o, layer, ln_g, ln_b, alpha, tm):
    T = x2.shape[0]
    rows = lambda width: pl.BlockSpec((tm, width), lambda i: (i, 0))
    weight = lambda r, c: pl.BlockSpec((None, r, c), lambda i: (layer, 0, 0))
    vec = pl.BlockSpec((1, D_MODEL), lambda i: (0, 0))
    return pl.pallas_call(
        functools.partial(_merge_kernel, alpha=alpha),
        grid=(T // tm,),
        in_specs=[
            rows(D_MODEL), rows(G_WIDTH), rows(R_WIDTH), rows(A_WIDTH),
            weight(D_MODEL, N_BRANCH * D_MODEL),
            weight(G_WIDTH, D_MODEL), weight(R_WIDTH, D_MODEL), weight(A_WIDTH, D_MODEL),
            weight(D_MODEL, D_MODEL), vec, vec,
        ],
        out_specs=rows(D_MODEL),
        out_shape=jax.ShapeDtypeStruct((T, D_MODEL), F32),
        compiler_params=_params(("parallel",)),
        name="merge",
    )(x2, yg, yr, ya, w_gate, w_pg, w_pr, w_pa, w_o, ln_g, ln_b)


def _ffn_kernel(x_ref, wg_ref, wu_ref, wd_ref, g_ref, b_ref, o_ref, *, alpha, ff_chunk):
    x = x_ref[...]
    xb = x.astype(BF16)
    out = None
    for c in range(D_FF // ff_chunk):
        cs = slice(c * ff_chunk, (c + 1) * ff_chunk)
        hg = _dot(xb, wg_ref[:, cs])
        hu = _dot(xb, wu_ref[:, cs])
        hid = (hg * _sigmoid(hg) * hu).astype(BF16)
        part = _dot(hid, wd_ref[cs, :])
        out = part if out is None else out + part
    o_ref[...] = _layer_norm(alpha * x + out, g_ref[...], b_ref[...])


def _ffn_call(x2, w_gate, w_up, w_down, layer, ln_g, ln_b, alpha, tm, ff_chunk):
    T = x2.shape[0]
    rows = pl.BlockSpec((tm, D_MODEL), lambda i: (i, 0))
    vec = pl.BlockSpec((1, D_MODEL), lambda i: (0, 0))
    return pl.pallas_call(
        functools.partial(_ffn_kernel, alpha=alpha, ff_chunk=ff_chunk),
        grid=(T // tm,),
        in_specs=[
            rows,
            pl.BlockSpec((None, D_MODEL, D_FF), lambda i: (layer, 0, 0)),
            pl.BlockSpec((None, D_MODEL, D_FF), lambda i: (layer, 0, 0)),
            pl.BlockSpec((None, D_FF, D_MODEL), lambda i: (layer, 0, 0)),
            vec, vec,
        ],
        out_specs=rows,
        out_shape=jax.ShapeDtypeStruct((T, D_MODEL), F32),
        compiler_params=_params(("parallel",)),
        name="ffn",
    )(x2, w_gate, w_up, w_down, ln_g, ln_b)


def kernel(x, w_in, gmlp_ln_g, gmlp_ln_b, gmlp_ws, gmlp_bs, hgrn_lb_logits, hgrn_norm_w, fox_fb,
           w_pg, w_pr, w_pa, w_o, ln1_g, ln1_b, w_gate, w_up, w_down, ln2_g, ln2_b):
    batch, seq, _ = x.shape
    depth = w_in.shape[0]
    T = batch * seq
    alpha = float((2 * depth) ** 0.25)
    rows = min(512, seq)
    tq = min(512, seq)
    tm = 256

    o_r = 2 * G_WIDTH
    o_a = o_r + 4 * R_WIDTH
    o_f = o_a + 3 * A_WIDTH
    o_gate = o_f + A_HEADS
    w_uv = w_in[:, :, :o_r].astype(BF16)
    w_r = w_in[:, :, o_r:o_a].astype(BF16)
    w_qkv = w_in[:, :, o_a:o_f].astype(BF16)
    w_f = jnp.pad(w_in[:, :, o_f:o_gate], ((0, 0), (0, 0), (0, LANES - A_HEADS))).astype(BF16)
    w_bgate = w_in[:, :, o_gate:].astype(BF16)
    w_pg_b, w_pr_b, w_pa_b, w_o_b = (w.astype(BF16) for w in (w_pg, w_pr, w_pa, w_o))
    w_gate_b, w_up_b, w_down_b = (w.astype(BF16) for w in (w_gate, w_up, w_down))

    probs = jax.nn.softmax(hgrn_lb_logits.astype(F32), axis=0)
    lbs = jnp.cumsum(probs, axis=0) - probs[0:1]
    fb_pad = jnp.pad(fox_fb.astype(F32), ((0, 0), (0, LANES - A_HEADS)))

    x2 = x.reshape(T, D_MODEL)
    for l in range(depth):
        bs_b = jnp.broadcast_to(gmlp_bs[l][:, :, None], (G_GROUPS, G_CHUNK, LANES))
        y_g = _gmlp_call(x2, w_uv, l, gmlp_ln_g[l][None], gmlp_ln_b[l][None], gmlp_ws[l], bs_b, tm)
        y_r = _hgrn_call(x2, w_r, l, lbs[l][None], hgrn_norm_w[l][None], batch, seq, rows)
        q, k0, k1, v0, v1 = _fox_proj_call(x2, w_qkv, w_f, l, fb_pad[l][None], batch, seq, rows)
        y_a = _fox_attn_call(q, k0, k1, v0, v1, batch, seq, tq)
        x2 = _merge_call(x2, y_g, y_r, y_a, w_bgate, w_pg_b, w_pr_b, w_pa_b, w_o_b, l,
                         ln1_g[l][None], ln1_b[l][None], alpha, tm)
        x2 = _ffn_call(x2, w_gate_b, w_up_b, w_down_b, l, ln2_g[l][None], ln2_b[l][None], alpha,
                       tm, D_FF // 2)
    return x2.reshape(batch, seq, D_MODEL)
```

```python
import functools

import numpy as np
import jax
import jax.numpy as jnp
from jax import lax
from jax.experimental import pallas as pl
from jax.experimental.pallas import tpu as pltpu

BF16 = jnp.bfloat16
F32 = jnp.float32

D_MODEL = 1024
G_GROUPS = 4
G_WIDTH = 512
G_CHUNK = 128
R_HEADS = 4
R_DK = 128
R_WIDTH = 512
A_HEADS = 8
A_HEAD_DIM = 64
A_WIDTH = 512
N_BRANCH = 3
D_FF = 2816
LN_EPS = 1e-5
RMS_EPS = 1e-6
NEG_BIG = -1e30
LB_FLOOR = 1e-30

LOG2E = float(np.log2(np.e))
LANES = 128
SUBLANES = 8
HGRN_CHUNK = 128
HGRN_LEVELS = 7
HGRN_SELECTOR_LEVELS = (1, 2)
VMEM_LIMIT_BYTES = 56 * 1024 * 1024


def _dot(a, b):
    return jnp.dot(a, b, preferred_element_type=F32)


def _dot_nt(a, b):
    return lax.dot_general(a, b, (((1,), (1,)), ((), ())), preferred_element_type=F32)


def _sigmoid(z):
    return 1.0 / (1.0 + jnp.exp(-z))


def _log_sigmoid(z):
    return jnp.minimum(z, 0.0) - jnp.log(1.0 + jnp.exp(-jnp.abs(z)))


def _gelu(z):
    return 0.5 * z * (1.0 + lax.erf(z * np.float32(2.0 ** -0.5)))


def _layer_norm(r, g, b):
    mu = jnp.mean(r, axis=-1, keepdims=True)
    d = r - mu
    var = jnp.mean(d * d, axis=-1, keepdims=True)
    return d * lax.rsqrt(var + LN_EPS) * g + b


def _split_bf16(a, terms):
    parts = []
    rem = a
    for _ in range(terms):
        p = rem.astype(BF16)
        parts.append(p)
        rem = rem - p.astype(F32)
    return parts


def _params(sem):
    return pltpu.CompilerParams(dimension_semantics=sem, vmem_limit_bytes=VMEM_LIMIT_BYTES)


def _gmlp_kernel(x_ref, w_ref, lng_ref, lnb_ref, ws_ref, bs_ref, o_ref):
    tm = x_ref.shape[0]
    h = _dot(x_ref[...].astype(BF16), w_ref[...])
    u = _gelu(h[:, :G_WIDTH])
    v = _layer_norm(_gelu(h[:, G_WIDTH:]), lng_ref[...], lnb_ref[...]).astype(BF16)
    row = lax.broadcasted_iota(jnp.int32, (G_CHUNK, G_CHUNK), 0)
    col = lax.broadcasted_iota(jnp.int32, (G_CHUNK, G_CHUNK), 1)
    for g in range(G_GROUPS):
        w_causal = jnp.where(row >= col, ws_ref[g], 0.0).astype(BF16)
        cs = slice(g * LANES, (g + 1) * LANES)
        for n in range(tm // G_CHUNK):
            rs = slice(n * G_CHUNK, (n + 1) * G_CHUNK)
            sv = _dot(w_causal, v[rs, cs]) + bs_ref[g]
            o_ref[rs, cs] = (u[rs, cs] * sv).astype(BF16)


def _gmlp_call(x2, w_uv, layer, ln_g, ln_b, ws, bs_b, tm):
    T = x2.shape[0]
    return pl.pallas_call(
        _gmlp_kernel,
        grid=(T // tm,),
        in_specs=[
            pl.BlockSpec((tm, D_MODEL), lambda i: (i, 0)),
            pl.BlockSpec((None, D_MODEL, 2 * G_WIDTH), lambda i: (layer, 0, 0)),
            pl.BlockSpec((1, G_WIDTH), lambda i: (0, 0)),
            pl.BlockSpec((1, G_WIDTH), lambda i: (0, 0)),
            pl.BlockSpec((G_GROUPS, G_CHUNK, G_CHUNK), lambda i: (0, 0, 0)),
            pl.BlockSpec((G_GROUPS, G_CHUNK, LANES), lambda i: (0, 0, 0)),
        ],
        out_specs=pl.BlockSpec((tm, G_WIDTH), lambda i: (i, 0)),
        out_shape=jax.ShapeDtypeStruct((T, G_WIDTH), BF16),
        compiler_params=_params(("parallel",)),
        name="gmlp",
    )(x2, w_uv, ln_g, ln_b, ws, bs_b)


def _hgrn_constants():
    c = HGRN_CHUNK
    t = np.arange(c)[:, None]
    tau = np.arange(c)[None, :]
    blocks = []
    for level in HGRN_SELECTOR_LEVELS:
        width = 1 << level
        start = (t >> level) * width
        end = start + width - 1
        odd = ((t >> level) & 1) == 1
        blocks.append(np.where(odd, (tau >= start) & (tau <= t), (tau > t) & (tau <= end)))
    blocks.append(tau <= t)
    selectors = np.concatenate(blocks, axis=0).astype(np.float32)
    xor = np.maximum(t ^ tau, 1)
    level_map = np.where(tau < t, np.floor(np.log2(xor)).astype(np.int32),
                         np.where(tau == t, -1, -2)).astype(np.int32)
    return selectors, level_map


def _hgrn_kernel(x_ref, w_ref, lb_ref, nw_ref, sel_ref, lev_ref, o_ref, proj_ref, state_ref):
    rows = x_ref.shape[0]
    c = HGRN_CHUNK

    @pl.when(pl.program_id(1) == 0)
    def _():
        state_ref[...] = jnp.zeros_like(state_ref)

    proj_ref[...] = _dot(x_ref[...].astype(BF16), w_ref[...])
    lb = lb_ref[...]
    log_lb = jnp.log(jnp.maximum(lb, LB_FLOOR))
    log_1m_lb = jnp.log1p(-lb)
    one_m_lb = 1.0 - lb
    norm_w = nw_ref[...]

    def odd_blocks(a, width):
        return jnp.concatenate([a[(2 * g + 1) * width:(2 * g + 2) * width] for g in range(c // (2 * width))], axis=0)

    def chunk(ci, carry):
        r0 = pl.multiple_of(ci * c, c)
        rs = pl.ds(r0, c)
        z = proj_ref[rs, R_WIDTH:2 * R_WIDTH]
        b = log_1m_lb + _log_sigmoid(z)
        lf = jnp.maximum(log_lb, b) + jnp.log(1.0 + jnp.exp(-jnp.abs(log_lb - b)))
        key = one_m_lb * _sigmoid(-z)
        lf = lf * np.float32(LOG2E)
        lf_hi, lf_lo = _split_bf16(lf, 2)
        sel = sel_ref[...]
        sums = _dot(sel, lf_hi) + _dot(sel, lf_lo)
        n_sel = len(HGRN_SELECTOR_LEVELS)
        cum = sums[n_sel * c:(n_sel + 1) * c]

        odd_row = (lax.broadcasted_iota(jnp.int32, lf.shape, 0) & 1) == 1
        log_decay = [jnp.where(odd_row, lf, 0.0)]
        for n, level in enumerate(HGRN_SELECTOR_LEVELS):
            log_decay.append(sums[n * c:(n + 1) * c])
        for level in range(len(log_decay), HGRN_LEVELS):
            width = 1 << level
            pieces = []
            for g in range(c // (2 * width)):
                base = 2 * g * width
                boundary = cum[base + width - 1:base + width]
                pieces.append(boundary - cum[base:base + width])
                pieces.append(cum[base + width:base + 2 * width] - boundary)
            log_decay.append(jnp.concatenate(pieces, axis=0))
        decay = [jnp.exp2(g) for g in log_decay]
        decay_b = [e.astype(BF16) for e in decay]
        e_in_all = jnp.exp2(cum)
        e_out_all = jnp.exp2(cum[c - 1:c] - cum)
        lev = lev_ref[...]
        for h in range(R_HEADS):
            hs = slice(h * LANES, (h + 1) * LANES)
            q = proj_ref[rs, h * LANES:(h + 1) * LANES]
            kh = key[:, hs]
            val = proj_ref[rs, 2 * R_WIDTH + h * LANES:2 * R_WIDTH + (h + 1) * LANES]
            gate = proj_ref[rs, 3 * R_WIDTH + h * LANES:3 * R_WIDTH + (h + 1) * LANES]
            q_b = q.astype(BF16)
            k_b = kh.astype(BF16)
            scores = jnp.zeros((c, c), F32)
            for level in range(HGRN_LEVELS):
                width = 1 << level
                k_l = k_b * decay_b[level][:, hs]
                if width < SUBLANES:
                    s_l = _dot_nt(q_b * decay_b[level][:, hs], k_l)
                    scores = jnp.where(lev == level, s_l, scores)
                else:
                    q_l = (odd_blocks(q, width) * odd_blocks(decay[level][:, hs], width)).astype(BF16)
                    s_l = _dot_nt(q_l, k_l)
                    rows_out = []
                    for g in range(c // (2 * width)):
                        base = 2 * g * width
                        odd = slice(base + width, base + 2 * width)
                        rows_out.append(scores[base:base + width])
                        rows_out.append(jnp.where(lev[odd] == level, s_l[g * width:(g + 1) * width], scores[odd]))
                    scores = jnp.concatenate(rows_out, axis=0)
            diag = jnp.sum(q * kh, axis=-1, keepdims=True)
            scores = jnp.where(lev == -1, diag, scores)
            val_b = val.astype(BF16)
            intra = _dot(scores.astype(BF16), val_b)
            e_in = e_in_all[:, hs]
            state = state_ref[h]
            inter = _dot_nt((q * e_in).astype(BF16), state.astype(BF16))
            e_last = e_in[c - 1:c, :]
            state_ref[h] = state * e_last + _dot(val.T.astype(BF16), (kh * e_out_all[:, hs]).astype(BF16))
            o = inter + intra
            o = o * lax.rsqrt(jnp.mean(o * o, axis=-1, keepdims=True) + RMS_EPS)
            o = o * norm_w * (gate * _sigmoid(gate))
            o_ref[rs, h * LANES:(h + 1) * LANES] = o.astype(BF16)
        return carry

    lax.fori_loop(0, rows // c, chunk, 0, unroll=True)


def _hgrn_call(x2, w_r, layer, lb, norm_w, batch, seq, rows):
    T = x2.shape[0]
    nr = seq // rows
    selectors, level_map = _hgrn_constants()
    n_sel = selectors.shape[0]
    return pl.pallas_call(
        _hgrn_kernel,
        grid=(batch, nr),
        in_specs=[
            pl.BlockSpec((rows, D_MODEL), lambda b, r: (b * nr + r, 0)),
            pl.BlockSpec((None, D_MODEL, 4 * R_WIDTH), lambda b, r: (layer, 0, 0)),
            pl.BlockSpec((1, R_WIDTH), lambda b, r: (0, 0)),
            pl.BlockSpec((1, R_DK), lambda b, r: (0, 0)),
            pl.BlockSpec((n_sel, HGRN_CHUNK), lambda b, r: (0, 0)),
            pl.BlockSpec((HGRN_CHUNK, HGRN_CHUNK), lambda b, r: (0, 0)),
        ],
        out_specs=pl.BlockSpec((rows, R_WIDTH), lambda b, r: (b * nr + r, 0)),
        out_shape=jax.ShapeDtypeStruct((T, R_WIDTH), BF16),
        scratch_shapes=[
            pltpu.VMEM((rows, 4 * R_WIDTH), F32),
            pltpu.VMEM((R_HEADS, R_DK, R_DK), F32),
        ],
        compiler_params=_params(("arbitrary", "arbitrary")),
        name="hgrn",
    )(x2, w_r, lb, norm_w, jnp.asarray(selectors, BF16), jnp.asarray(level_map))


FOX_F_TERMS = 3


def _fox_placement():
    place = np.zeros((FOX_F_TERMS * LANES, 2 * A_WIDTH), np.float32)
    for h in range(A_HEADS):
        pair, odd = divmod(h, 2)
        spare0 = pair * LANES + (0 if odd else A_HEAD_DIM)
        for t in range(FOX_F_TERMS):
            place[t * LANES + h, odd * A_WIDTH + spare0 + t] = 1.0
    return place


def _fox_proj_kernel(x_ref, wk_ref, wqv_ref, wf_ref, fb_ref, tri_ref, place_ref,
                     q0_ref, q1_ref, k0_ref, k1_ref, v0_ref, v1_ref, carry_ref):
    rows = x_ref.shape[0]

    @pl.when(pl.program_id(1) == 0)
    def _():
        carry_ref[...] = jnp.zeros_like(carry_ref)

    xb = x_ref[...].astype(BF16)
    k = _dot(xb, wk_ref[...])
    qv_t = _dot_nt(wqv_ref[...], xb)
    q_t = qv_t[:A_WIDTH] * np.float32(A_HEAD_DIM ** -0.5 * LOG2E)
    v_t = qv_t[A_WIDTH:]

    log_f = _log_sigmoid(_dot(xb, wf_ref[...]) + fb_ref[...])
    tri = tri_ref[...]
    p1, p2, p3 = _split_bf16(log_f, 3)
    cum = (_dot(tri, p3) + _dot(tri, p2)) + _dot(tri, p1) + carry_ref[...]
    carry_ref[...] = cum[rows - 1:rows, :]
    pieces = jnp.concatenate(_split_bf16(cum * np.float32(-LOG2E), FOX_F_TERMS), axis=1)
    placed = _dot(pieces, place_ref[...])

    even_lane = (lax.broadcasted_iota(jnp.int32, (rows, A_WIDTH), 1) & A_HEAD_DIM) == 0
    k0_ref[...] = jnp.where(even_lane, k, placed[:, :A_WIDTH]).astype(BF16)
    k1_ref[...] = jnp.where(even_lane, placed[:, A_WIDTH:], k).astype(BF16)

    feature = lax.broadcasted_iota(jnp.int32, (A_WIDTH, rows), 0)
    even_row = (feature & A_HEAD_DIM) == 0
    slot = feature & (A_HEAD_DIM - 1)
    f_ones = jnp.where(slot < FOX_F_TERMS, 1.0, 0.0)
    q0_ref[...] = jnp.where(even_row, q_t, f_ones).astype(BF16)
    q1_ref[...] = jnp.where(even_row, f_ones, q_t).astype(BF16)
    v0_ref[...] = jnp.where(even_row, v_t, 1.0).astype(BF16)
    v1_ref[...] = jnp.where(even_row, 1.0, v_t).astype(BF16)


def _fox_proj_call(x2, w_k, w_qv_t, w_f, layer, fb, batch, seq, rows):
    T = x2.shape[0]
    nr = seq // rows
    tri = jnp.asarray(np.tril(np.ones((rows, rows), np.float32)), BF16)
    place = jnp.asarray(_fox_placement(), BF16)
    row_major = jax.ShapeDtypeStruct((T, A_WIDTH), BF16)
    transposed = jax.ShapeDtypeStruct((batch, A_WIDTH, seq), BF16)
    row_block = lambda width: pl.BlockSpec((rows, width), lambda b, r: (b * nr + r, 0))
    col_block = pl.BlockSpec((None, A_WIDTH, rows), lambda b, r: (b, 0, r))
    return pl.pallas_call(
        _fox_proj_kernel,
        grid=(batch, nr),
        in_specs=[
            row_block(D_MODEL),
            pl.BlockSpec((None, D_MODEL, A_WIDTH), lambda b, r: (layer, 0, 0)),
            pl.BlockSpec((None, 2 * A_WIDTH, D_MODEL), lambda b, r: (layer, 0, 0)),
            pl.BlockSpec((None, D_MODEL, LANES), lambda b, r: (layer, 0, 0)),
            pl.BlockSpec((1, LANES), lambda b, r: (0, 0)),
            pl.BlockSpec((rows, rows), lambda b, r: (0, 0)),
            pl.BlockSpec(place.shape, lambda b, r: (0, 0)),
        ],
        out_specs=[col_block, col_block, row_block(A_WIDTH), row_block(A_WIDTH), col_block, col_block],
        out_shape=[transposed, transposed, row_major, row_major, transposed, transposed],
        scratch_shapes=[pltpu.VMEM((1, LANES), F32)],
        compiler_params=_params(("arbitrary", "arbitrary")),
        name="fox_proj",
    )(x2, w_k, w_qv_t, w_f, fb, tri, place)


def _fox_attn_kernel(q0_ref, q1_ref, k0_ref, k1_ref, v0_ref, v1_ref, o_ref, x_ref, m_ref, acc_ref, *, tile):
    seq = o_ref.shape[0]
    nq = seq // tile
    total = nq * (nq + 1) // 2
    key_pos = lax.broadcasted_iota(jnp.int32, (tile, tile), 0)
    qry_pos = lax.broadcasted_iota(jnp.int32, (tile, tile), 1)
    q_refs = (q0_ref, q1_ref)
    k_refs = (k0_ref, k1_ref)
    v_refs = (v0_ref, v1_ref)

    def scores(i, j, slot):
        qc = pl.multiple_of(i * tile, tile)
        kc = pl.multiple_of(j * tile, tile)
        for h in range(2):
            x_ref[slot, h] = _dot(k_refs[h][pl.ds(kc, tile), :], q_refs[h][:, pl.ds(qc, tile)])

    def accumulate(j, slot, masked):
        kc = pl.multiple_of(j * tile, tile)
        for h in range(2):
            m = m_ref[h]
            x = x_ref[slot, h]
            if masked:
                x = jnp.where(key_pos <= qry_pos, x, NEG_BIG)
            m_new = jnp.maximum(m, jnp.max(x, axis=0, keepdims=True))
            p = jnp.exp2(x - m_new).astype(BF16)
            acc_ref[h] = jnp.exp2(m - m_new) * acc_ref[h] + _dot(v_refs[h][:, pl.ds(kc, tile)], p)
            m_ref[h] = m_new

    def reset():
        m_ref[...] = jnp.full(m_ref.shape, NEG_BIG, F32)
        acc_ref[...] = jnp.zeros(acc_ref.shape, F32)

    def finalize(i):
        qc = pl.multiple_of(i * tile, tile)
        a0 = acc_ref[0]
        a1 = acc_ref[1]
        out_t = jnp.concatenate([a0[:A_HEAD_DIM] / a0[A_HEAD_DIM:A_HEAD_DIM + 1],
                                 a1[A_HEAD_DIM:] / a1[0:1]], axis=0)
        o_ref[pl.ds(qc, tile), :] = out_t.T.astype(BF16)
        reset()

    def advance(i, j):
        wrap = j >= i
        return jnp.where(wrap, i + 1, i), jnp.where(wrap, 0, j + 1)

    def half(t, i, j, slot):
        ni, nj = advance(i, j)
        ci = jnp.minimum(ni, nq - 1)
        cj = jnp.minimum(nj, ci)
        valid = t < total

        @pl.when(valid & (j < i))
        def _():
            scores(ci, cj, 1 - slot)
            accumulate(j, slot, masked=False)

        @pl.when(valid & (j == i))
        def _():
            scores(ci, cj, 1 - slot)
            accumulate(j, slot, masked=True)
            finalize(i)

        return ni, nj

    def body(u, carry):
        i, j = carry
        i, j = half(2 * u, i, j, 0)
        return half(2 * u + 1, i, j, 1)

    reset()
    scores(0, 0, 0)
    lax.fori_loop(0, (total + 1) // 2, body, (jnp.int32(0), jnp.int32(0)))


def _fox_attn_call(q0, q1, k0, k1, v0, v1, batch, seq, tile):
    T = batch * seq
    pairs = A_HEADS // 2
    row_major = pl.BlockSpec((seq, LANES), lambda b, p: (b, p))
    transposed = pl.BlockSpec((None, LANES, seq), lambda b, p: (b, p, 0))
    return pl.pallas_call(
        functools.partial(_fox_attn_kernel, tile=tile),
        grid=(batch, pairs),
        in_specs=[transposed, transposed, row_major, row_major, transposed, transposed],
        out_specs=row_major,
        out_shape=jax.ShapeDtypeStruct((T, A_WIDTH), BF16),
        scratch_shapes=[
            pltpu.VMEM((2, 2, tile, tile), F32),
            pltpu.VMEM((2, 1, tile), F32),
            pltpu.VMEM((2, LANES, tile), F32),
        ],
        compiler_params=_params(("parallel", "parallel")),
        name="fox_attn",
    )(q0, q1, k0, k1, v0, v1)


def _merge_kernel(x_ref, yg_ref, yr_ref, ya_ref, wgate_ref, wpg_ref, wpr_ref, wpa_ref, wo_ref,
                  g_ref, b_ref, o_ref, *, alpha):
    x = x_ref[...]
    xb = x.astype(BF16)
    mixed = None
    for n, (y_ref, wp_ref) in enumerate(((yg_ref, wpg_ref), (yr_ref, wpr_ref), (ya_ref, wpa_ref))):
        gate = _sigmoid(_dot(xb, wgate_ref[:, n * D_MODEL:(n + 1) * D_MODEL]))
        term = gate * _dot(y_ref[...], wp_ref[...])
        mixed = term if mixed is None else mixed + term
    out = _dot(mixed.astype(BF16), wo_ref[...])
    o_ref[...] = _layer_norm(alpha * x + out, g_ref[...], b_ref[...])


def _merge_call(x2, yg, yr, ya, w_gate, w_pg, w_pr, w_pa, w_o, layer, ln_g, ln_b, alpha, tm):
    T = x2.shape[0]
    rows = lambda width: pl.BlockSpec((tm, width), lambda i: (i, 0))
    weight = lambda r, c: pl.BlockSpec((None, r, c), lambda i: (layer, 0, 0))
    vec = pl.BlockSpec((1, D_MODEL), lambda i: (0, 0))
    return pl.pallas_call(
        functools.partial(_merge_kernel, alpha=alpha),
        grid=(T // tm,),
        in_specs=[
            rows(D_MODEL), rows(G_WIDTH), rows(R_WIDTH), rows(A_WIDTH),
            weight(D_MODEL, N_BRANCH * D_MODEL),
            weight(G_WIDTH, D_MODEL), weight(R_WIDTH, D_MODEL), weight(A_WIDTH, D_MODEL),
            weight(D_MODEL, D_MODEL), vec, vec,
        ],
        out_specs=rows(D_MODEL),
        out_shape=jax.ShapeDtypeStruct((T, D_MODEL), F32),
        compiler_params=_params(("parallel",)),
        name="merge",
    )(x2, yg, yr, ya, w_gate, w_pg, w_pr, w_pa, w_o, ln_g, ln_b)


def _ffn_kernel(x_ref, wg_ref, wu_ref, wd_ref, g_ref, b_ref, o_ref, *, alpha, ff_chunk):
    x = x_ref[...]
    xb = x.astype(BF16)
    out = None
    for c in range(D_FF // ff_chunk):
        cs = slice(c * ff_chunk, (c + 1) * ff_chunk)
        hg = _dot(xb, wg_ref[:, cs])
        hu = _dot(xb, wu_ref[:, cs])
        hid = (hg * _sigmoid(hg) * hu).astype(BF16)
        part = _dot(hid, wd_ref[cs, :])
        out = part if out is None else out + part
    o_ref[...] = _layer_norm(alpha * x + out, g_ref[...], b_ref[...])


def _ffn_call(x2, w_gate, w_up, w_down, layer, ln_g, ln_b, alpha, tm, ff_chunk):
    T = x2.shape[0]
    rows = pl.BlockSpec((tm, D_MODEL), lambda i: (i, 0))
    vec = pl.BlockSpec((1, D_MODEL), lambda i: (0, 0))
    return pl.pallas_call(
        functools.partial(_ffn_kernel, alpha=alpha, ff_chunk=ff_chunk),
        grid=(T // tm,),
        in_specs=[
            rows,
            pl.BlockSpec((None, D_MODEL, D_FF), lambda i: (layer, 0, 0)),
            pl.BlockSpec((None, D_MODEL, D_FF), lambda i: (layer, 0, 0)),
            pl.BlockSpec((None, D_FF, D_MODEL), lambda i: (layer, 0, 0)),
            vec, vec,
        ],
        out_specs=rows,
        out_shape=jax.ShapeDtypeStruct((T, D_MODEL), F32),
        compiler_params=_params(("parallel",)),
        name="ffn",
    )(x2, w_gate, w_up, w_down, ln_g, ln_b)


def kernel(x, w_in, gmlp_ln_g, gmlp_ln_b, gmlp_ws, gmlp_bs, hgrn_lb_logits, hgrn_norm_w, fox_fb,
           w_pg, w_pr, w_pa, w_o, ln1_g, ln1_b, w_gate, w_up, w_down, ln2_g, ln2_b):
    batch, seq, _ = x.shape
    depth = w_in.shape[0]
    T = batch * seq
    alpha = float((2 * depth) ** 0.25)
    rows = min(512, seq)
    tq = min(512, seq)
    tm = 256

    o_r = 2 * G_WIDTH
    o_a = o_r + 4 * R_WIDTH
    o_f = o_a + 3 * A_WIDTH
    o_gate = o_f + A_HEADS
    w_uv = w_in[:, :, :o_r].astype(BF16)
    w_r = w_in[:, :, o_r:o_a].astype(BF16)
    w_k = w_in[:, :, o_a + A_WIDTH:o_a + 2 * A_WIDTH].astype(BF16)
    w_qv_t = jnp.concatenate([w_in[:, :, o_a:o_a + A_WIDTH], w_in[:, :, o_a + 2 * A_WIDTH:o_f]],
                             axis=2).transpose(0, 2, 1).astype(BF16)
    w_f =jnp.pad(w_in[:, :, o_f:o_gate], ((0, 0), (0, 0), (0, LANES - A_HEADS))).astype(BF16)
    w_bgate = w_in[:, :, o_gate:].astype(BF16)
    w_pg_b, w_pr_b, w_pa_b, w_o_b = (w.astype(BF16) for w in (w_pg, w_pr, w_pa, w_o))
    w_gate_b, w_up_b, w_down_b = (w.astype(BF16) for w in (w_gate, w_up, w_down))

    probs = jax.nn.softmax(hgrn_lb_logits.astype(F32), axis=0)
    lbs = jnp.cumsum(probs, axis=0) - probs[0:1]
    fb_pad = jnp.pad(fox_fb.astype(F32), ((0, 0), (0, LANES - A_HEADS)))

    x2 = x.reshape(T, D_MODEL)
    for l in range(depth):
        bs_b = jnp.broadcast_to(gmlp_bs[l][:, :, None], (G_GROUPS, G_CHUNK, LANES))
        y_g = _gmlp_call(x2, w_uv, l, gmlp_ln_g[l][None], gmlp_ln_b[l][None], gmlp_ws[l], bs_b, tm)
        y_r = _hgrn_call(x2, w_r, l, lbs[l][None], hgrn_norm_w[l][None], batch, seq, rows)
        q0, q1, k0, k1, v0, v1 = _fox_proj_call(x2, w_k, w_qv_t, w_f, l, fb_pad[l][None], batch, seq, rows)
        y_a = _fox_attn_call(q0, q1, k0, k1, v0, v1, batch, seq, tq)
        x2 = _merge_call(x2, y_g, y_r, y_a, w_bgate, w_pg_b, w_pr_b, w_pa_b, w_o_b, l,
                         ln1_g[l][None], ln1_b[l][None], alpha, tm)
        x2 = _ffn_call(x2, w_gate_b, w_up_b, w_down_b, l, ln2_g[l][None], ln2_b[l][None], alpha,
                       tm, D_FF // 2)
    return x2.reshape(batch, seq, D_MODEL)
```

```python
import functools

import numpy as np
import jax
import jax.numpy as jnp
from jax import lax
from jax.experimental import pallas as pl
from jax.experimental.pallas import tpu as pltpu

BF16 = jnp.bfloat16
F32 = jnp.float32

D_MODEL = 1024
G_GROUPS = 4
G_WIDTH = 512
G_CHUNK = 128
R_HEADS = 4
R_DK = 128
R_WIDTH = 512
A_HEADS = 8
A_HEAD_DIM = 64
A_WIDTH = 512
N_BRANCH = 3
D_FF = 2816
LN_EPS = 1e-5
RMS_EPS = 1e-6
NEG_BIG = -1e30
LB_FLOOR = 1e-30

LOG2E = float(np.log2(np.e))
LANES = 128
SUBLANES = 8
HGRN_CHUNK = 128
HGRN_LEVELS = 7
HGRN_SELECTOR_LEVELS = (1, 2)
VMEM_LIMIT_BYTES = 56 * 1024 * 1024


def _dot(a, b):
    return jnp.dot(a, b, preferred_element_type=F32)


def _dot_nt(a, b):
    return lax.dot_general(a, b, (((1,), (1,)), ((), ())), preferred_element_type=F32)


def _sigmoid(z):
    return 1.0 / (1.0 + jnp.exp(-z))


def _log_sigmoid(z):
    return jnp.minimum(z, 0.0) - jnp.log(1.0 + jnp.exp(-jnp.abs(z)))


def _gelu(z):
    return 0.5 * z * (1.0 + lax.erf(z * np.float32(2.0 ** -0.5)))


def _layer_norm(r, g, b):
    mu = jnp.mean(r, axis=-1, keepdims=True)
    d = r - mu
    var = jnp.mean(d * d, axis=-1, keepdims=True)
    return d * lax.rsqrt(var + LN_EPS) * g + b


def _split_bf16(a, terms):
    parts = []
    rem = a
    for _ in range(terms):
        p = rem.astype(BF16)
        parts.append(p)
        rem = rem - p.astype(F32)
    return parts


def _layer_weight(layer, rows, cols, grid_rank):
    index_map = (lambda i: (layer, 0, 0)) if grid_rank == 1 else (lambda b, r: (layer, 0, 0))
    return pl.BlockSpec((None, rows, cols), index_map, pipeline_mode=pl.Buffered(1))


def _params(sem):
    return pltpu.CompilerParams(dimension_semantics=sem, vmem_limit_bytes=VMEM_LIMIT_BYTES)


def _gmlp_kernel(x_ref, w_ref, lng_ref, lnb_ref, ws_ref, bs_ref, o_ref):
    tm = x_ref.shape[0]
    h = _dot(x_ref[...].astype(BF16), w_ref[...])
    u = _gelu(h[:, :G_WIDTH])
    v = _layer_norm(_gelu(h[:, G_WIDTH:]), lng_ref[...], lnb_ref[...]).astype(BF16)
    row = lax.broadcasted_iota(jnp.int32, (G_CHUNK, G_CHUNK), 0)
    col = lax.broadcasted_iota(jnp.int32, (G_CHUNK, G_CHUNK), 1)
    for g in range(G_GROUPS):
        w_causal = jnp.where(row >= col, ws_ref[g], 0.0).astype(BF16)
        cs = slice(g * LANES, (g + 1) * LANES)
        for n in range(tm // G_CHUNK):
            rs = slice(n * G_CHUNK, (n + 1) * G_CHUNK)
            sv = _dot(w_causal, v[rs, cs]) + bs_ref[g]
            o_ref[rs, cs] = (u[rs, cs] * sv).astype(BF16)


def _gmlp_call(x2, w_uv, layer, ln_g, ln_b, ws, bs_b, tm):
    T = x2.shape[0]
    return pl.pallas_call(
        _gmlp_kernel,
        grid=(T // tm,),
        in_specs=[
            pl.BlockSpec((tm, D_MODEL), lambda i: (i, 0)),
            _layer_weight(layer, D_MODEL, 2 * G_WIDTH, 1),
            pl.BlockSpec((1, G_WIDTH), lambda i: (0, 0)),
            pl.BlockSpec((1, G_WIDTH), lambda i: (0, 0)),
            pl.BlockSpec((G_GROUPS, G_CHUNK, G_CHUNK), lambda i: (0, 0, 0)),
            pl.BlockSpec((G_GROUPS, G_CHUNK, LANES), lambda i: (0, 0, 0)),
        ],
        out_specs=pl.BlockSpec((tm, G_WIDTH), lambda i: (i, 0)),
        out_shape=jax.ShapeDtypeStruct((T, G_WIDTH), BF16),
        compiler_params=_params(("parallel",)),
        name="gmlp",
    )(x2, w_uv, ln_g, ln_b, ws, bs_b)


def _hgrn_constants():
    c = HGRN_CHUNK
    t = np.arange(c)[:, None]
    tau = np.arange(c)[None, :]
    blocks = []
    for level in HGRN_SELECTOR_LEVELS:
        width = 1 << level
        start = (t >> level) * width
        end = start + width - 1
        odd = ((t >> level) & 1) == 1
        blocks.append(np.where(odd, (tau >= start) & (tau <= t), (tau > t) & (tau <= end)))
    blocks.append(tau <= t)
    selectors = np.concatenate(blocks, axis=0).astype(np.float32)
    xor = np.maximum(t ^ tau, 1)
    level_map = np.where(tau < t, np.floor(np.log2(xor)).astype(np.int32),
                         np.where(tau == t, -1, -2)).astype(np.int32)
    return selectors, level_map


def _hgrn_kernel(x_ref, w_ref, lb_ref, nw_ref, sel_ref, lev_ref, o_ref, proj_ref, state_ref):
    rows = x_ref.shape[0]
    c = HGRN_CHUNK

    @pl.when(pl.program_id(1) == 0)
    def _():
        state_ref[...] = jnp.zeros_like(state_ref)

    proj_ref[...] = _dot(x_ref[...].astype(BF16), w_ref[...])
    lb = lb_ref[...]
    log_lb = jnp.log(jnp.maximum(lb, LB_FLOOR))
    log_1m_lb = jnp.log1p(-lb)
    one_m_lb = 1.0 - lb
    norm_w = nw_ref[...]

    def odd_blocks(a, width):
        return jnp.concatenate([a[(2 * g + 1) * width:(2 * g + 2) * width] for g in range(c // (2 * width))], axis=0)

    def chunk(ci, carry):
        r0 = pl.multiple_of(ci * c, c)
        rs = pl.ds(r0, c)
        z = proj_ref[rs, R_WIDTH:2 * R_WIDTH]
        b = log_1m_lb + _log_sigmoid(z)
        lf = jnp.maximum(log_lb, b) + jnp.log(1.0 + jnp.exp(-jnp.abs(log_lb - b)))
        key = one_m_lb * _sigmoid(-z)
        lf = lf * np.float32(LOG2E)
        lf_hi, lf_lo = _split_bf16(lf, 2)
        sel = sel_ref[...]
        sums = _dot(sel, lf_hi) + _dot(sel, lf_lo)
        n_sel = len(HGRN_SELECTOR_LEVELS)
        cum = sums[n_sel * c:(n_sel + 1) * c]

        odd_row = (lax.broadcasted_iota(jnp.int32, lf.shape, 0) & 1) == 1
        log_decay = [jnp.where(odd_row, lf, 0.0)]
        for n, level in enumerate(HGRN_SELECTOR_LEVELS):
            log_decay.append(sums[n * c:(n + 1) * c])
        for level in range(len(log_decay), HGRN_LEVELS):
            width = 1 << level
            pieces = []
            for g in range(c // (2 * width)):
                base = 2 * g * width
                boundary = cum[base + width - 1:base + width]
                pieces.append(boundary - cum[base:base + width])
                pieces.append(cum[base + width:base + 2 * width] - boundary)
            log_decay.append(jnp.concatenate(pieces, axis=0))
        decay = [jnp.exp2(g) for g in log_decay]
        decay_b = [e.astype(BF16) for e in decay]
        e_in_all = jnp.exp2(cum)
        e_out_all = jnp.exp2(cum[c - 1:c] - cum)
        lev = lev_ref[...]
        for h in range(R_HEADS):
            hs = slice(h * LANES, (h + 1) * LANES)
            q = proj_ref[rs, h * LANES:(h + 1) * LANES]
            kh = key[:, hs]
            val = proj_ref[rs, 2 * R_WIDTH + h * LANES:2 * R_WIDTH + (h + 1) * LANES]
            gate = proj_ref[rs, 3 * R_WIDTH + h * LANES:3 * R_WIDTH + (h + 1) * LANES]
            q_b = q.astype(BF16)
            k_b = kh.astype(BF16)
            scores = jnp.zeros((c, c), F32)
            for level in range(HGRN_LEVELS):
                width = 1 << level
                k_l = k_b * decay_b[level][:, hs]
                if width < SUBLANES:
                    s_l = _dot_nt(q_b * decay_b[level][:, hs], k_l)
                    scores = jnp.where(lev == level, s_l, scores)
                else:
                    q_l = (odd_blocks(q, width) * odd_blocks(decay[level][:, hs], width)).astype(BF16)
                    s_l = _dot_nt(q_l, k_l)
                    rows_out = []
                    for g in range(c // (2 * width)):
                        base = 2 * g * width
                        odd = slice(base + width, base + 2 * width)
                        rows_out.append(scores[base:base + width])
                        rows_out.append(jnp.where(lev[odd] == level, s_l[g * width:(g + 1) * width], scores[odd]))
                    scores = jnp.concatenate(rows_out, axis=0)
            diag = jnp.sum(q * kh, axis=-1, keepdims=True)
            scores = jnp.where(lev == -1, diag, scores)
            val_b = val.astype(BF16)
            intra = _dot(scores.astype(BF16), val_b)
            e_in = e_in_all[:, hs]
            state = state_ref[h]
            inter = _dot_nt((q * e_in).astype(BF16), state.astype(BF16))
            e_last = e_in[c - 1:c, :]
            state_ref[h] = state * e_last + _dot(val.T.astype(BF16), (kh * e_out_all[:, hs]).astype(BF16))
            o = inter + intra
            o = o * lax.rsqrt(jnp.mean(o * o, axis=-1, keepdims=True) + RMS_EPS)
            o = o * norm_w * (gate * _sigmoid(gate))
            o_ref[rs, h * LANES:(h + 1) * LANES] = o.astype(BF16)
        return carry

    lax.fori_loop(0, rows // c, chunk, 0, unroll=True)


def _hgrn_call(x2, w_r, layer, lb, norm_w, batch, seq, rows):
    T = x2.shape[0]
    nr = seq // rows
    selectors, level_map = _hgrn_constants()
    n_sel = selectors.shape[0]
    return pl.pallas_call(
        _hgrn_kernel,
        grid=(batch, nr),
        in_specs=[
            pl.BlockSpec((rows, D_MODEL), lambda b, r: (b * nr + r, 0)),
            _layer_weight(layer, D_MODEL, 4 * R_WIDTH, 2),
            pl.BlockSpec((1, R_WIDTH), lambda b, r: (0, 0)),
            pl.BlockSpec((1, R_DK), lambda b, r: (0, 0)),
            pl.BlockSpec((n_sel, HGRN_CHUNK), lambda b, r: (0, 0)),
            pl.BlockSpec((HGRN_CHUNK, HGRN_CHUNK), lambda b, r: (0, 0)),
        ],
        out_specs=pl.BlockSpec((rows, R_WIDTH), lambda b, r: (b * nr + r, 0)),
        out_shape=jax.ShapeDtypeStruct((T, R_WIDTH), BF16),
        scratch_shapes=[
            pltpu.VMEM((rows, 4 * R_WIDTH), F32),
            pltpu.VMEM((R_HEADS, R_DK, R_DK), F32),
        ],
        compiler_params=_params(("arbitrary", "arbitrary")),
        name="hgrn",
    )(x2, w_r, lb, norm_w, jnp.asarray(selectors, BF16), jnp.asarray(level_map))


FOX_F_TERMS = 3


def _fox_placement():
    place = np.zeros((FOX_F_TERMS * LANES, 2 * A_WIDTH), np.float32)
    for h in range(A_HEADS):
        pair, odd = divmod(h, 2)
        spare0 = pair * LANES + (0 if odd else A_HEAD_DIM)
        for t in range(FOX_F_TERMS):
            place[t * LANES + h, odd * A_WIDTH + spare0 + t] = 1.0
    return place


def _fox_proj_kernel(x_ref, wk_ref, wqv_ref, wf_ref, fb_ref, tri_ref, place_ref,
                     q0_ref, q1_ref, k0_ref, k1_ref, v0_ref, v1_ref, carry_ref):
    rows = x_ref.shape[0]

    @pl.when(pl.program_id(1) == 0)
    def _():
        carry_ref[...] = jnp.zeros_like(carry_ref)

    xb = x_ref[...].astype(BF16)
    k = _dot(xb, wk_ref[...])
    qv_t = _dot_nt(wqv_ref[...], xb)
    q_t = qv_t[:A_WIDTH] * np.float32(A_HEAD_DIM ** -0.5 * LOG2E)
    v_t = qv_t[A_WIDTH:]

    log_f = _log_sigmoid(_dot(xb, wf_ref[...]) + fb_ref[...])
    tri = tri_ref[...]
    p1, p2, p3 = _split_bf16(log_f, 3)
    cum = (_dot(tri, p3) + _dot(tri, p2)) + _dot(tri, p1) + carry_ref[...]
    carry_ref[...] = cum[rows - 1:rows, :]
    pieces = jnp.concatenate(_split_bf16(cum * np.float32(-LOG2E), FOX_F_TERMS), axis=1)
    placed = _dot(pieces, place_ref[...])

    even_lane = (lax.broadcasted_iota(jnp.int32, (rows, A_WIDTH), 1) & A_HEAD_DIM) == 0
    k0_ref[...] = jnp.where(even_lane, k, placed[:, :A_WIDTH]).astype(BF16)
    k1_ref[...] = jnp.where(even_lane, placed[:, A_WIDTH:], k).astype(BF16)

    feature = lax.broadcasted_iota(jnp.int32, (A_WIDTH, rows), 0)
    even_row = (feature & A_HEAD_DIM) == 0
    slot = feature & (A_HEAD_DIM - 1)
    f_ones = jnp.where(slot < FOX_F_TERMS, 1.0, 0.0)
    q0_ref[...] = jnp.where(even_row, q_t, f_ones).astype(BF16)
    q1_ref[...] = jnp.where(even_row, f_ones, q_t).astype(BF16)
    v0_ref[...] = jnp.where(even_row, v_t, 1.0).astype(BF16)
    v1_ref[...] = jnp.where(even_row, 1.0, v_t).astype(BF16)


def _fox_proj_call(x2, w_k, w_qv_t, w_f, layer, fb, batch, seq, rows):
    T = x2.shape[0]
    nr = seq // rows
    tri = jnp.asarray(np.tril(np.ones((rows, rows), np.float32)), BF16)
    place = jnp.asarray(_fox_placement(), BF16)
    row_major = jax.ShapeDtypeStruct((T, A_WIDTH), BF16)
    transposed = jax.ShapeDtypeStruct((batch, A_WIDTH, seq), BF16)
    row_block = lambda width: pl.BlockSpec((rows, width), lambda b, r: (b * nr + r, 0))
    col_block = pl.BlockSpec((None, A_WIDTH, rows), lambda b, r: (b, 0, r))
    return pl.pallas_call(
        _fox_proj_kernel,
        grid=(batch, nr),
        in_specs=[
            row_block(D_MODEL),
            _layer_weight(layer, D_MODEL, A_WIDTH, 2),
            _layer_weight(layer, 2 * A_WIDTH, D_MODEL, 2),
            _layer_weight(layer, D_MODEL, LANES, 2),
            pl.BlockSpec((1, LANES), lambda b, r: (0, 0)),
            pl.BlockSpec((rows, rows), lambda b, r: (0, 0)),
            pl.BlockSpec(place.shape, lambda b, r: (0, 0)),
        ],
        out_specs=[col_block, col_block, row_block(A_WIDTH), row_block(A_WIDTH), col_block, col_block],
        out_shape=[transposed, transposed, row_major, row_major, transposed, transposed],
        scratch_shapes=[pltpu.VMEM((1, LANES), F32)],
        compiler_params=_params(("arbitrary", "arbitrary")),
        name="fox_proj",
    )(x2, w_k, w_qv_t, w_f, fb, tri, place)


def _fox_attn_kernel(q0_ref, q1_ref, k0_ref, k1_ref, v0_ref, v1_ref, o_ref, x_ref, m_ref, acc_ref, *, tile):
    seq = o_ref.shape[0]
    nq = seq // tile
    total = nq * (nq + 1) // 2
    key_pos = lax.broadcasted_iota(jnp.int32, (tile, tile), 0)
    qry_pos = lax.broadcasted_iota(jnp.int32, (tile, tile), 1)
    q_refs = (q0_ref, q1_ref)
    k_refs = (k0_ref, k1_ref)
    v_refs = (v0_ref, v1_ref)

    def scores(i, j, slot):
        qc = pl.multiple_of(i * tile, tile)
        kc = pl.multiple_of(j * tile, tile)
        for h in range(2):
            x_ref[slot, h] = _dot(k_refs[h][pl.ds(kc, tile), :], q_refs[h][:, pl.ds(qc, tile)])

    def accumulate(j, slot, masked):
        kc = pl.multiple_of(j * tile, tile)
        for h in range(2):
            m = m_ref[h]
            x = x_ref[slot, h]
            if masked:
                x = jnp.where(key_pos <= qry_pos, x, NEG_BIG)
            m_new = jnp.maximum(m, jnp.max(x, axis=0, keepdims=True))
            p = jnp.exp2(x - m_new).astype(BF16)
            acc_ref[h] = jnp.exp2(m - m_new) * acc_ref[h] + _dot(v_refs[h][:, pl.ds(kc, tile)], p)
            m_ref[h] = m_new

    def reset():
        m_ref[...] = jnp.full(m_ref.shape, NEG_BIG, F32)
        acc_ref[...] = jnp.zeros(acc_ref.shape, F32)

    def finalize(i):
        qc = pl.multiple_of(i * tile, tile)
        a0 = acc_ref[0]
        a1 = acc_ref[1]
        out_t = jnp.concatenate([a0[:A_HEAD_DIM] / a0[A_HEAD_DIM:A_HEAD_DIM + 1],
                                 a1[A_HEAD_DIM:] / a1[0:1]], axis=0)
        o_ref[pl.ds(qc, tile), :] = out_t.T.astype(BF16)
        reset()

    def advance(i, j):
        wrap = j >= i
        return jnp.where(wrap, i + 1, i), jnp.where(wrap, 0, j + 1)

    def half(t, i, j, slot):
        ni, nj = advance(i, j)
        ci = jnp.minimum(ni, nq - 1)
        cj = jnp.minimum(nj, ci)
        valid = t < total

        @pl.when(valid & (j < i))
        def _():
            scores(ci, cj, 1 - slot)
            accumulate(j, slot, masked=False)

        @pl.when(valid & (j == i))
        def _():
            scores(ci, cj, 1 - slot)
            accumulate(j, slot, masked=True)
            finalize(i)

        return ni, nj

    def body(u, carry):
        i, j = carry
        i, j = half(2 * u, i, j, 0)
        return half(2 * u + 1, i, j, 1)

    reset()
    scores(0, 0, 0)
    lax.fori_loop(0, (total + 1) // 2, body, (jnp.int32(0), jnp.int32(0)))


def _fox_attn_call(q0, q1, k0, k1, v0, v1, batch, seq, tile):
    T = batch * seq
    pairs = A_HEADS // 2
    row_major = pl.BlockSpec((seq, LANES), lambda b, p: (b, p))
    transposed = pl.BlockSpec((None, LANES, seq), lambda b, p: (b, p, 0))
    return pl.pallas_call(
        functools.partial(_fox_attn_kernel, tile=tile),
        grid=(batch, pairs),
        in_specs=[transposed, transposed, row_major, row_major, transposed, transposed],
        out_specs=row_major,
        out_shape=jax.ShapeDtypeStruct((T, A_WIDTH), BF16),
        scratch_shapes=[
            pltpu.VMEM((2, 2, tile, tile), F32),
            pltpu.VMEM((2, 1, tile), F32),
            pltpu.VMEM((2, LANES, tile), F32),
        ],
        compiler_params=_params(("parallel", "parallel")),
        name="fox_attn",
    )(q0, q1, k0, k1, v0, v1)


def _merge_kernel(x_ref, yg_ref, yr_ref, ya_ref, wgate_ref, wpg_ref, wpr_ref, wpa_ref, wo_ref,
                  g_ref, b_ref, o_ref, *, alpha):
    x = x_ref[...]
    xb = x.astype(BF16)
    mixed = None
    for n, (y_ref, wp_ref) in enumerate(((yg_ref, wpg_ref), (yr_ref, wpr_ref), (ya_ref, wpa_ref))):
        gate = _sigmoid(_dot(xb, wgate_ref[:, n * D_MODEL:(n + 1) * D_MODEL]))
        term = gate * _dot(y_ref[...], wp_ref[...])
        mixed = term if mixed is None else mixed + term
    out = _dot(mixed.astype(BF16), wo_ref[...])
    o_ref[...] = _layer_norm(alpha * x + out, g_ref[...], b_ref[...])


def _merge_call(x2, yg, yr, ya, w_gate, w_pg, w_pr, w_pa, w_o, layer, ln_g, ln_b, alpha, tm):
    T = x2.shape[0]
    rows = lambda width: pl.BlockSpec((tm, width), lambda i: (i, 0))
    weight = lambda r, c: _layer_weight(layer, r, c, 1)
    vec = pl.BlockSpec((1, D_MODEL), lambda i: (0, 0))
    return pl.pallas_call(
        functools.partial(_merge_kernel, alpha=alpha),
        grid=(T // tm,),
        in_specs=[
            rows(D_MODEL), rows(G_WIDTH), rows(R_WIDTH), rows(A_WIDTH),
            weight(D_MODEL, N_BRANCH * D_MODEL),
            weight(G_WIDTH, D_MODEL), weight(R_WIDTH, D_MODEL), weight(A_WIDTH, D_MODEL),
            weight(D_MODEL, D_MODEL), vec, vec,
        ],
        out_specs=rows(D_MODEL),
        out_shape=jax.ShapeDtypeStruct((T, D_MODEL), F32),
        compiler_params=_params(("parallel",)),
        name="merge",
    )(x2, yg, yr, ya, w_gate, w_pg, w_pr, w_pa, w_o, ln_g, ln_b)


def _ffn_kernel(x_ref, wg_ref, wu_ref, wd_ref, g_ref, b_ref, o_ref, *, alpha, ff_chunk):
    x = x_ref[...]
    xb = x.astype(BF16)
    out = None
    for c in range(D_FF // ff_chunk):
        cs = slice(c * ff_chunk, (c + 1) * ff_chunk)
        hg = _dot(xb, wg_ref[:, cs])
        hu = _dot(xb, wu_ref[:, cs])
        hid = (hg * _sigmoid(hg) * hu).astype(BF16)
        part = _dot(hid, wd_ref[cs, :])
        out = part if out is None else out + part
    o_ref[...] = _layer_norm(alpha * x + out, g_ref[...], b_ref[...])


def _ffn_call(x2, w_gate, w_up, w_down, layer, ln_g, ln_b, alpha, tm, ff_chunk):
    T = x2.shape[0]
    rows = pl.BlockSpec((tm, D_MODEL), lambda i: (i, 0))
    vec = pl.BlockSpec((1, D_MODEL), lambda i: (0, 0))
    return pl.pallas_call(
        functools.partial(_ffn_kernel, alpha=alpha, ff_chunk=ff_chunk),
        grid=(T // tm,),
        in_specs=[
            rows,
            _layer_weight(layer, D_MODEL, D_FF, 1),
            _layer_weight(layer, D_MODEL, D_FF, 1),
            _layer_weight(layer, D_FF, D_MODEL, 1),
            vec, vec,
        ],
        out_specs=rows,
        out_shape=jax.ShapeDtypeStruct((T, D_MODEL), F32),
        compiler_params=_params(("parallel",)),
        name="ffn",
    )(x2, w_gate, w_up, w_down, ln_g, ln_b)


def kernel(x, w_in, gmlp_ln_g, gmlp_ln_b, gmlp_ws, gmlp_bs, hgrn_lb_logits, hgrn_norm_w, fox_fb,
           w_pg, w_pr, w_pa, w_o, ln1_g, ln1_b, w_gate, w_up, w_down, ln2_g, ln2_b):
    batch, seq, _ = x.shape
    depth = w_in.shape[0]
    T = batch * seq
    alpha = float((2 * depth) ** 0.25)
    rows = min(512, seq)
    tq = min(512, seq)
    tm = 512

    o_r = 2 * G_WIDTH
    o_a = o_r + 4 * R_WIDTH
    o_f = o_a + 3 * A_WIDTH
    o_gate = o_f + A_HEADS
    w_uv = w_in[:, :, :o_r].astype(BF16)
    w_r = w_in[:, :, o_r:o_a].astype(BF16)
    w_k = w_in[:, :, o_a + A_WIDTH:o_a + 2 * A_WIDTH].astype(BF16)
    w_qv_t = jnp.concatenate([w_in[:, :, o_a:o_a + A_WIDTH], w_in[:, :, o_a + 2 * A_WIDTH:o_f]],
                             axis=2).transpose(0, 2, 1).astype(BF16)
    w_f =jnp.pad(w_in[:, :, o_f:o_gate], ((0, 0), (0, 0), (0, LANES - A_HEADS))).astype(BF16)
    w_bgate = w_in[:, :, o_gate:].astype(BF16)
    w_pg_b, w_pr_b, w_pa_b, w_o_b = (w.astype(BF16) for w in (w_pg, w_pr, w_pa, w_o))
    w_gate_b, w_up_b, w_down_b = (w.astype(BF16) for w in (w_gate, w_up, w_down))

    probs = jax.nn.softmax(hgrn_lb_logits.astype(F32), axis=0)
    lbs = jnp.cumsum(probs, axis=0) - probs[0:1]
    fb_pad = jnp.pad(fox_fb.astype(F32), ((0, 0), (0, LANES - A_HEADS)))

    x2 = x.reshape(T, D_MODEL)
    for l in range(depth):
        bs_b = jnp.broadcast_to(gmlp_bs[l][:, :, None], (G_GROUPS, G_CHUNK, LANES))
        y_g = _gmlp_call(x2, w_uv, l, gmlp_ln_g[l][None], gmlp_ln_b[l][None], gmlp_ws[l], bs_b, tm)
        y_r = _hgrn_call(x2, w_r, l, lbs[l][None], hgrn_norm_w[l][None], batch, seq, rows)
        q0, q1, k0, k1, v0, v1 = _fox_proj_call(x2, w_k, w_qv_t, w_f, l, fb_pad[l][None], batch, seq, rows)
        y_a = _fox_attn_call(q0, q1, k0, k1, v0, v1, batch, seq, tq)
        x2 = _merge_call(x2, y_g, y_r, y_a, w_bgate, w_pg_b, w_pr_b, w_pa_b, w_o_b, l,
                         ln1_g[l][None], ln1_b[l][None], alpha, tm)
        x2 = _ffn_call(x2, w_gate_b, w_up_b, w_down_b, l, ln2_g[l][None], ln2_b[l][None], alpha,
                       tm, D_FF)
    return x2.reshape(batch, seq, D_MODEL)
```

```python
import functools

import numpy as np
import jax
import jax.numpy as jnp
from jax import lax
from jax.experimental import pallas as pl
from jax.experimental.pallas import tpu as pltpu

BF16 = jnp.bfloat16
F32 = jnp.float32

D_MODEL = 1024
G_GROUPS = 4
G_WIDTH = 512
G_CHUNK = 128
R_HEADS = 4
R_DK = 128
R_WIDTH = 512
A_HEADS = 8
A_HEAD_DIM = 64
A_WIDTH = 512
N_BRANCH = 3
D_FF = 2816
LN_EPS = 1e-5
RMS_EPS = 1e-6
NEG_BIG = -1e30
LB_FLOOR = 1e-30

LOG2E = float(np.log2(np.e))
LANES = 128
SUBLANES = 8
HGRN_CHUNK = 128
HGRN_LEVELS = 7
HGRN_SELECTOR_LEVELS = (1, 2)
VMEM_LIMIT_BYTES = 56 * 1024 * 1024


def _dot(a, b):
    return jnp.dot(a, b, preferred_element_type=F32)


def _dot_nt(a, b):
    return lax.dot_general(a, b, (((1,), (1,)), ((), ())), preferred_element_type=F32)


def _sigmoid(z):
    return 1.0 / (1.0 + jnp.exp(-z))


def _log_sigmoid(z):
    return jnp.minimum(z, 0.0) - jnp.log(1.0 + jnp.exp(-jnp.abs(z)))


def _gelu(z):
    return 0.5 * z * (1.0 + lax.erf(z * np.float32(2.0 ** -0.5)))


def _layer_norm(r, g, b):
    mu = jnp.mean(r, axis=-1, keepdims=True)
    d = r - mu
    var = jnp.mean(d * d, axis=-1, keepdims=True)
    return d * lax.rsqrt(var + LN_EPS) * g + b


def _split_bf16(a, terms):
    parts = []
    rem = a
    for _ in range(terms):
        p = rem.astype(BF16)
        parts.append(p)
        rem = rem - p.astype(F32)
    return parts


def _layer_weight(layer, rows, cols, grid_rank):
    index_map = (lambda i: (layer, 0, 0)) if grid_rank == 1 else (lambda b, r: (layer, 0, 0))
    return pl.BlockSpec((None, rows, cols), index_map, pipeline_mode=pl.Buffered(1))


def _params(sem):
    return pltpu.CompilerParams(dimension_semantics=sem, vmem_limit_bytes=VMEM_LIMIT_BYTES)


def _gmlp_kernel(x_ref, w_ref, lng_ref, lnb_ref, ws_ref, bs_ref, o_ref):
    tm = x_ref.shape[0]
    h = _dot(x_ref[...].astype(BF16), w_ref[...])
    u = _gelu(h[:, :G_WIDTH])
    v = _layer_norm(_gelu(h[:, G_WIDTH:]), lng_ref[...], lnb_ref[...]).astype(BF16)
    row = lax.broadcasted_iota(jnp.int32, (G_CHUNK, G_CHUNK), 0)
    col = lax.broadcasted_iota(jnp.int32, (G_CHUNK, G_CHUNK), 1)
    for g in range(G_GROUPS):
        w_causal = jnp.where(row >= col, ws_ref[g], 0.0).astype(BF16)
        cs = slice(g * LANES, (g + 1) * LANES)
        for n in range(tm // G_CHUNK):
            rs = slice(n * G_CHUNK, (n + 1) * G_CHUNK)
            sv = _dot(w_causal, v[rs, cs]) + bs_ref[g]
            o_ref[rs, cs] = (u[rs, cs] * sv).astype(BF16)


def _gmlp_call(x2, w_uv, layer, ln_g, ln_b, ws, bs_b, tm):
    T = x2.shape[0]
    return pl.pallas_call(
        _gmlp_kernel,
        grid=(T // tm,),
        in_specs=[
            pl.BlockSpec((tm, D_MODEL), lambda i: (i, 0)),
            _layer_weight(layer, D_MODEL, 2 * G_WIDTH, 1),
            pl.BlockSpec((1, G_WIDTH), lambda i: (0, 0)),
            pl.BlockSpec((1, G_WIDTH), lambda i: (0, 0)),
            pl.BlockSpec((G_GROUPS, G_CHUNK, G_CHUNK), lambda i: (0, 0, 0)),
            pl.BlockSpec((G_GROUPS, G_CHUNK, LANES), lambda i: (0, 0, 0)),
        ],
        out_specs=pl.BlockSpec((tm, G_WIDTH), lambda i: (i, 0)),
        out_shape=jax.ShapeDtypeStruct((T, G_WIDTH), BF16),
        compiler_params=_params(("parallel",)),
        name="gmlp",
    )(x2, w_uv, ln_g, ln_b, ws, bs_b)


def _hgrn_constants():
    c = HGRN_CHUNK
    t = np.arange(c)[:, None]
    tau = np.arange(c)[None, :]
    blocks = []
    for level in HGRN_SELECTOR_LEVELS:
        width = 1 << level
        start = (t >> level) * width
        end = start + width - 1
        odd = ((t >> level) & 1) == 1
        blocks.append(np.where(odd, (tau >= start) & (tau <= t), (tau > t) & (tau <= end)))
    blocks.append(tau <= t)
    selectors = np.concatenate(blocks, axis=0).astype(np.float32)
    xor = np.maximum(t ^ tau, 1)
    level_map = np.where(tau < t, np.floor(np.log2(xor)).astype(np.int32),
                         np.where(tau == t, -1, -2)).astype(np.int32)
    return selectors, level_map


def _hgrn_kernel(x_ref, w_ref, lb_ref, nw_ref, sel_ref, lev_ref, o_ref, proj_ref, state_ref):
    rows = x_ref.shape[0]
    c = HGRN_CHUNK

    @pl.when(pl.program_id(1) == 0)
    def _():
        state_ref[...] = jnp.zeros_like(state_ref)

    proj_ref[...] = _dot(x_ref[...].astype(BF16), w_ref[...])
    lb = lb_ref[...]
    log_lb = jnp.log(jnp.maximum(lb, LB_FLOOR))
    log_1m_lb = jnp.log1p(-lb)
    one_m_lb = 1.0 - lb
    norm_w = nw_ref[...]

    def odd_blocks(a, width):
        return jnp.concatenate([a[(2 * g + 1) * width:(2 * g + 2) * width] for g in range(c // (2 * width))], axis=0)

    def chunk(ci, carry):
        r0 = pl.multiple_of(ci * c, c)
        rs = pl.ds(r0, c)
        z = proj_ref[rs, R_WIDTH:2 * R_WIDTH]
        b = log_1m_lb + _log_sigmoid(z)
        lf = jnp.maximum(log_lb, b) + jnp.log(1.0 + jnp.exp(-jnp.abs(log_lb - b)))
        key = one_m_lb * _sigmoid(-z)
        lf = lf * np.float32(LOG2E)
        lf_hi, lf_lo = _split_bf16(lf, 2)
        sel = sel_ref[...]
        sums = _dot(sel, lf_hi) + _dot(sel, lf_lo)
        n_sel = len(HGRN_SELECTOR_LEVELS)
        cum = sums[n_sel * c:(n_sel + 1) * c]

        odd_row = (lax.broadcasted_iota(jnp.int32, lf.shape, 0) & 1) == 1
        log_decay = [jnp.where(odd_row, lf, 0.0)]
        for n, level in enumerate(HGRN_SELECTOR_LEVELS):
            log_decay.append(sums[n * c:(n + 1) * c])
        for level in range(len(log_decay), HGRN_LEVELS):
            width = 1 << level
            pieces = []
            for g in range(c // (2 * width)):
                base = 2 * g * width
                boundary = cum[base + width - 1:base + width]
                pieces.append(boundary - cum[base:base + width])
                pieces.append(cum[base + width:base + 2 * width] - boundary)
            log_decay.append(jnp.concatenate(pieces, axis=0))
        decay = [jnp.exp2(g) for g in log_decay]
        decay_b = [e.astype(BF16) for e in decay]
        e_in_all = jnp.exp2(cum)
        e_out_all = jnp.exp2(cum[c - 1:c] - cum)
        lev = lev_ref[...]
        for h in range(R_HEADS):
            hs = slice(h * LANES, (h + 1) * LANES)
            q = proj_ref[rs, h * LANES:(h + 1) * LANES]
            kh = key[:, hs]
            val = proj_ref[rs, 2 * R_WIDTH + h * LANES:2 * R_WIDTH + (h + 1) * LANES]
            gate = proj_ref[rs, 3 * R_WIDTH + h * LANES:3 * R_WIDTH + (h + 1) * LANES]
            q_b = q.astype(BF16)
            k_b = kh.astype(BF16)
            scores = jnp.zeros((c, c), F32)
            for level in range(HGRN_LEVELS):
                width = 1 << level
                k_l = k_b * decay_b[level][:, hs]
                if width < SUBLANES:
                    s_l = _dot_nt(q_b * decay_b[level][:, hs], k_l)
                    scores = jnp.where(lev == level, s_l, scores)
                else:
                    q_l = (odd_blocks(q, width) * odd_blocks(decay[level][:, hs], width)).astype(BF16)
                    s_l = _dot_nt(q_l, k_l)
                    rows_out = []
                    for g in range(c // (2 * width)):
                        base = 2 * g * width
                        odd = slice(base + width, base + 2 * width)
                        rows_out.append(scores[base:base + width])
                        rows_out.append(jnp.where(lev[odd] == level, s_l[g * width:(g + 1) * width], scores[odd]))
                    scores = jnp.concatenate(rows_out, axis=0)
            diag = jnp.sum(q * kh, axis=-1, keepdims=True)
            scores = jnp.where(lev == -1, diag, scores)
            val_b = val.astype(BF16)
            intra = _dot(scores.astype(BF16), val_b)
            e_in = e_in_all[:, hs]
            state = state_ref[h]
            inter = _dot_nt((q * e_in).astype(BF16), state.astype(BF16))
            e_last = e_in[c - 1:c, :]
            state_ref[h] = state * e_last + _dot(val.T.astype(BF16), (kh * e_out_all[:, hs]).astype(BF16))
            o = inter + intra
            o = o * lax.rsqrt(jnp.mean(o * o, axis=-1, keepdims=True) + RMS_EPS)
            o = o * norm_w * (gate * _sigmoid(gate))
            o_ref[rs, h * LANES:(h + 1) * LANES] = o.astype(BF16)
        return carry

    lax.fori_loop(0, rows // c, chunk, 0, unroll=True)


def _hgrn_call(x2, w_r, layer, lb, norm_w, batch, seq, rows):
    T = x2.shape[0]
    nr = seq // rows
    selectors, level_map = _hgrn_constants()
    n_sel = selectors.shape[0]
    return pl.pallas_call(
        _hgrn_kernel,
        grid=(batch, nr),
        in_specs=[
            pl.BlockSpec((rows, D_MODEL), lambda b, r: (b * nr + r, 0)),
            _layer_weight(layer, D_MODEL, 4 * R_WIDTH, 2),
            pl.BlockSpec((1, R_WIDTH), lambda b, r: (0, 0)),
            pl.BlockSpec((1, R_DK), lambda b, r: (0, 0)),
            pl.BlockSpec((n_sel, HGRN_CHUNK), lambda b, r: (0, 0)),
            pl.BlockSpec((HGRN_CHUNK, HGRN_CHUNK), lambda b, r: (0, 0)),
        ],
        out_specs=pl.BlockSpec((rows, R_WIDTH), lambda b, r: (b * nr + r, 0)),
        out_shape=jax.ShapeDtypeStruct((T, R_WIDTH), BF16),
        scratch_shapes=[
            pltpu.VMEM((rows, 4 * R_WIDTH), F32),
            pltpu.VMEM((R_HEADS, R_DK, R_DK), F32),
        ],
        compiler_params=_params(("arbitrary", "arbitrary")),
        name="hgrn",
    )(x2, w_r, lb, norm_w, jnp.asarray(selectors, BF16), jnp.asarray(level_map))


FOX_F_TERMS = 3


def _fox_placement():
    place = np.zeros((LANES, 2 * A_WIDTH), np.float32)
    for h in range(A_HEADS):
        pair, odd = divmod(h, 2)
        spare0 = pair * LANES + (0 if odd else A_HEAD_DIM)
        for t in range(FOX_F_TERMS):
            place[t * A_HEADS + h, odd * A_WIDTH + spare0 + t] = 1.0
    return place


def _fox_pack_pieces(a, lane):
    packed = jnp.zeros(a.shape, F32)
    for t, piece in reversed(list(enumerate(_split_bf16(a, FOX_F_TERMS)))):
        packed = jnp.where(lane < (t + 1) * A_HEADS, piece.astype(F32), packed)
    return packed.astype(BF16)


def _fox_proj_kernel(x_ref, wk_ref, wqv_ref, wf_ref, fb_ref, tri_ref, place_ref,
                     q0_ref, q1_ref, k0_ref, k1_ref, v0_ref, v1_ref, carry_ref):
    rows = x_ref.shape[0]

    @pl.when(pl.program_id(1) == 0)
    def _():
        carry_ref[...] = jnp.zeros_like(carry_ref)

    xb = x_ref[...].astype(BF16)
    k = _dot(xb, wk_ref[...])
    qv_t = _dot_nt(wqv_ref[...], xb)
    q_t = qv_t[:A_WIDTH] * np.float32(A_HEAD_DIM ** -0.5 * LOG2E)
    v_t = qv_t[A_WIDTH:]

    lane = lax.broadcasted_iota(jnp.int32, (rows, LANES), 1)
    log_f = _log_sigmoid(_dot(xb, wf_ref[...]) + fb_ref[...])
    part = _dot(tri_ref[...], _fox_pack_pieces(log_f, lane))
    cum = part
    for t in range(1, FOX_F_TERMS):
        cum = cum + pltpu.roll(part, t * A_HEADS, axis=1) + pltpu.roll(part, LANES - t * A_HEADS, axis=1)
    cum = cum + carry_ref[...]
    carry_ref[...] = cum[rows - 1:rows, :]
    placed = _dot(_fox_pack_pieces(cum * np.float32(-LOG2E), lane), place_ref[...])

    even_lane = (lax.broadcasted_iota(jnp.int32, (rows, A_WIDTH), 1) & A_HEAD_DIM) == 0
    k0_ref[...] = jnp.where(even_lane, k, placed[:, :A_WIDTH]).astype(BF16)
    k1_ref[...] = jnp.where(even_lane, placed[:, A_WIDTH:], k).astype(BF16)

    feature = lax.broadcasted_iota(jnp.int32, (A_WIDTH, rows), 0)
    even_row = (feature & A_HEAD_DIM) == 0
    slot = feature & (A_HEAD_DIM - 1)
    f_ones = jnp.where(slot < FOX_F_TERMS, 1.0, 0.0)
    q0_ref[...] = jnp.where(even_row, q_t, f_ones).astype(BF16)
    q1_ref[...] = jnp.where(even_row, f_ones, q_t).astype(BF16)
    v0_ref[...] = jnp.where(even_row, v_t, 1.0).astype(BF16)
    v1_ref[...] = jnp.where(even_row, 1.0, v_t).astype(BF16)


def _fox_proj_call(x2, w_k, w_qv_t, w_f, layer, fb, batch, seq, rows):
    T = x2.shape[0]
    nr = seq // rows
    tri = jnp.asarray(np.tril(np.ones((rows, rows), np.float32)), BF16)
    place = jnp.asarray(_fox_placement(), BF16)
    row_major = jax.ShapeDtypeStruct((T, A_WIDTH), BF16)
    transposed = jax.ShapeDtypeStruct((batch, A_WIDTH, seq), BF16)
    row_block = lambda width: pl.BlockSpec((rows, width), lambda b, r: (b * nr + r, 0))
    col_block = pl.BlockSpec((None, A_WIDTH, rows), lambda b, r: (b, 0, r))
    return pl.pallas_call(
        _fox_proj_kernel,
        grid=(batch, nr),
        in_specs=[
            row_block(D_MODEL),
            _layer_weight(layer, D_MODEL, A_WIDTH, 2),
            _layer_weight(layer, 2 * A_WIDTH, D_MODEL, 2),
            _layer_weight(layer, D_MODEL, LANES, 2),
            pl.BlockSpec((1, LANES), lambda b, r: (0, 0)),
            pl.BlockSpec((rows, rows), lambda b, r: (0, 0)),
            pl.BlockSpec(place.shape, lambda b, r: (0, 0)),
        ],
        out_specs=[col_block, col_block, row_block(A_WIDTH), row_block(A_WIDTH), col_block, col_block],
        out_shape=[transposed, transposed, row_major, row_major, transposed, transposed],
        scratch_shapes=[pltpu.VMEM((1, LANES), F32)],
        compiler_params=_params(("arbitrary", "arbitrary")),
        name="fox_proj",
    )(x2, w_k, w_qv_t, w_f, fb, tri, place)


def _fox_attn_kernel(q0_ref, q1_ref, k0_ref, k1_ref, v0_ref, v1_ref, o_ref,
                     x_ref, xmax_ref, m_ref, acc_ref, *, tile):
    seq = o_ref.shape[0]
    nq = seq // tile
    total = nq * (nq + 1) // 2
    key_pos = lax.broadcasted_iota(jnp.int32, (tile, tile), 0)
    qry_pos = lax.broadcasted_iota(jnp.int32, (tile, tile), 1)
    q_refs = (q0_ref, q1_ref)
    k_refs = (k0_ref, k1_ref)
    v_refs = (v0_ref, v1_ref)

    def scores(i, j, slot, masked):
        qc = pl.multiple_of(i * tile, tile)
        kc = pl.multiple_of(j * tile, tile)
        for h in range(2):
            x = _dot(k_refs[h][pl.ds(kc, tile), :], q_refs[h][:, pl.ds(qc, tile)])
            if masked:
                x = jnp.where(key_pos <= qry_pos, x, NEG_BIG)
            x_ref[slot, h] = x
            xmax_ref[slot, h] = jnp.max(x, axis=0, keepdims=True)

    def accumulate(j, slot):
        kc = pl.multiple_of(j * tile, tile)
        for h in range(2):
            m = m_ref[h]
            m_new = jnp.maximum(m, xmax_ref[slot, h])
            p = jnp.exp2((x_ref[slot, h] - m_new).astype(BF16))
            acc_ref[h] = jnp.exp2(m - m_new) * acc_ref[h] + _dot(v_refs[h][:, pl.ds(kc, tile)], p)
            m_ref[h] = m_new

    def reset():
        m_ref[...] = jnp.full(m_ref.shape, NEG_BIG, F32)
        acc_ref[...] = jnp.zeros(acc_ref.shape, F32)

    def finalize(i):
        qc = pl.multiple_of(i * tile, tile)
        a0 = acc_ref[0]
        a1 = acc_ref[1]
        out_t = jnp.concatenate([a0[:A_HEAD_DIM] / a0[A_HEAD_DIM:A_HEAD_DIM + 1],
                                 a1[A_HEAD_DIM:] / a1[0:1]], axis=0)
        o_ref[pl.ds(qc, tile), :] = out_t.T.astype(BF16)
        reset()

    def advance(i, j):
        wrap = j >= i
        return jnp.where(wrap, i + 1, i), jnp.where(wrap, 0, j + 1)

    def half(t, i, j, slot):
        ni, nj = advance(i, j)
        ci = jnp.minimum(ni, nq - 1)
        cj = jnp.minimum(nj, ci)
        valid = t < total

        for next_diagonal in (False, True):
            @pl.when(valid & ((cj == ci) == next_diagonal))
            def _(next_diagonal=next_diagonal):
                scores(ci, cj, 1 - slot, masked=next_diagonal)
                accumulate(j, slot)

        @pl.when(valid & (j == i))
        def _():
            finalize(i)

        return ni, nj

    def body(u, carry):
        i, j = carry
        i, j = half(2 * u, i, j, 0)
        return half(2 * u + 1, i, j, 1)

    reset()
    scores(0, 0, 0, masked=True)
    lax.fori_loop(0, (total + 1) // 2, body, (jnp.int32(0), jnp.int32(0)))


def _fox_attn_call(q0, q1, k0, k1, v0, v1, batch, seq, tile):
    T = batch * seq
    pairs = A_HEADS // 2
    row_major = pl.BlockSpec((seq, LANES), lambda b, p: (b, p))
    transposed = pl.BlockSpec((None, LANES, seq), lambda b, p: (b, p, 0))
    return pl.pallas_call(
        functools.partial(_fox_attn_kernel, tile=tile),
        grid=(batch, pairs),
        in_specs=[transposed, transposed, row_major, row_major, transposed, transposed],
        out_specs=row_major,
        out_shape=jax.ShapeDtypeStruct((T, A_WIDTH), BF16),
        scratch_shapes=[
            pltpu.VMEM((2, 2, tile, tile), F32),
            pltpu.VMEM((2, 2, 1, tile), F32),
            pltpu.VMEM((2, 1, tile), F32),
            pltpu.VMEM((2, LANES, tile), F32),
        ],
        compiler_params=_params(("parallel", "parallel")),
        name="fox_attn",
    )(q0, q1, k0, k1, v0, v1)


def _merge_kernel(x_ref, yg_ref, yr_ref, ya_ref, wgate_ref, wpg_ref, wpr_ref, wpa_ref, wo_ref,
                  g_ref, b_ref, o_ref, *, alpha):
    x = x_ref[...]
    xb = x.astype(BF16)
    mixed = None
    for n, (y_ref, wp_ref) in enumerate(((yg_ref, wpg_ref), (yr_ref, wpr_ref), (ya_ref, wpa_ref))):
        gate = _sigmoid(_dot(xb, wgate_ref[:, n * D_MODEL:(n + 1) * D_MODEL]))
        term = gate * _dot(y_ref[...], wp_ref[...])
        mixed = term if mixed is None else mixed + term
    out = _dot(mixed.astype(BF16), wo_ref[...])
    o_ref[...] = _layer_norm(alpha * x + out, g_ref[...], b_ref[...])


def _merge_call(x2, yg, yr, ya, w_gate, w_pg, w_pr, w_pa, w_o, layer, ln_g, ln_b, alpha, tm):
    T = x2.shape[0]
    rows = lambda width: pl.BlockSpec((tm, width), lambda i: (i, 0))
    weight = lambda r, c: _layer_weight(layer, r, c, 1)
    vec = pl.BlockSpec((1, D_MODEL), lambda i: (0, 0))
    return pl.pallas_call(
        functools.partial(_merge_kernel, alpha=alpha),
        grid=(T // tm,),
        in_specs=[
            rows(D_MODEL), rows(G_WIDTH), rows(R_WIDTH), rows(A_WIDTH),
            weight(D_MODEL, N_BRANCH * D_MODEL),
            weight(G_WIDTH, D_MODEL), weight(R_WIDTH, D_MODEL), weight(A_WIDTH, D_MODEL),
            weight(D_MODEL, D_MODEL), vec, vec,
        ],
        out_specs=rows(D_MODEL),
        out_shape=jax.ShapeDtypeStruct((T, D_MODEL), F32),
        compiler_params=_params(("parallel",)),
        name="merge",
    )(x2, yg, yr, ya, w_gate, w_pg, w_pr, w_pa, w_o, ln_g, ln_b)


def _ffn_kernel(x_ref, wg_ref, wu_ref, wd_ref, g_ref, b_ref, o_ref, *, alpha, ff_chunk):
    x = x_ref[...]
    xb = x.astype(BF16)
    out = None
    for c in range(D_FF // ff_chunk):
        cs = slice(c * ff_chunk, (c + 1) * ff_chunk)
        hg = _dot(xb, wg_ref[:, cs])
        hu = _dot(xb, wu_ref[:, cs])
        hid = (hg * _sigmoid(hg) * hu).astype(BF16)
        part = _dot(hid, wd_ref[cs, :])
        out = part if out is None else out + part
    o_ref[...] = _layer_norm(alpha * x + out, g_ref[...], b_ref[...])


def _ffn_call(x2, w_gate, w_up, w_down, layer, ln_g, ln_b, alpha, tm, ff_chunk):
    T = x2.shape[0]
    rows = pl.BlockSpec((tm, D_MODEL), lambda i: (i, 0))
    vec = pl.BlockSpec((1, D_MODEL), lambda i: (0, 0))
    return pl.pallas_call(
        functools.partial(_ffn_kernel, alpha=alpha, ff_chunk=ff_chunk),
        grid=(T // tm,),
        in_specs=[
            rows,
            _layer_weight(layer, D_MODEL, D_FF, 1),
            _layer_weight(layer, D_MODEL, D_FF, 1),
            _layer_weight(layer, D_FF, D_MODEL, 1),
            vec, vec,
        ],
        out_specs=rows,
        out_shape=jax.ShapeDtypeStruct((T, D_MODEL), F32),
        compiler_params=_params(("parallel",)),
        name="ffn",
    )(x2, w_gate, w_up, w_down, ln_g, ln_b)


def kernel(x, w_in, gmlp_ln_g, gmlp_ln_b, gmlp_ws, gmlp_bs, hgrn_lb_logits, hgrn_norm_w, fox_fb,
           w_pg, w_pr, w_pa, w_o, ln1_g, ln1_b, w_gate, w_up, w_down, ln2_g, ln2_b):
    batch, seq, _ = x.shape
    depth = w_in.shape[0]
    T = batch * seq
    alpha = float((2 * depth) ** 0.25)
    rows = min(512, seq)
    tq = min(512, seq)
    tm = 512

    o_r = 2 * G_WIDTH
    o_a = o_r + 4 * R_WIDTH
    o_f = o_a + 3 * A_WIDTH
    o_gate = o_f + A_HEADS
    w_uv = w_in[:, :, :o_r].astype(BF16)
    w_r = w_in[:, :, o_r:o_a].astype(BF16)
    w_k = w_in[:, :, o_a + A_WIDTH:o_a + 2 * A_WIDTH].astype(BF16)
    w_qv_t = jnp.concatenate([w_in[:, :, o_a:o_a + A_WIDTH], w_in[:, :, o_a + 2 * A_WIDTH:o_f]],
                             axis=2).transpose(0, 2, 1).astype(BF16)
    spare = LANES - FOX_F_TERMS * A_HEADS
    w_f = jnp.pad(jnp.tile(w_in[:, :, o_f:o_gate], (1, 1, FOX_F_TERMS)), ((0, 0), (0, 0), (0, spare))).astype(BF16)
    w_bgate = w_in[:, :, o_gate:].astype(BF16)
    w_pg_b, w_pr_b, w_pa_b, w_o_b = (w.astype(BF16) for w in (w_pg, w_pr, w_pa, w_o))
    w_gate_b, w_up_b, w_down_b = (w.astype(BF16) for w in (w_gate, w_up, w_down))

    probs = jax.nn.softmax(hgrn_lb_logits.astype(F32), axis=0)
    lbs = jnp.cumsum(probs, axis=0) - probs[0:1]
    fb_pad = jnp.pad(jnp.tile(fox_fb.astype(F32), (1, FOX_F_TERMS)), ((0, 0), (0, spare)))

    x2 = x.reshape(T, D_MODEL)
    for l in range(depth):
        bs_b = jnp.broadcast_to(gmlp_bs[l][:, :, None], (G_GROUPS, G_CHUNK, LANES))
        y_g = _gmlp_call(x2, w_uv, l, gmlp_ln_g[l][None], gmlp_ln_b[l][None], gmlp_ws[l], bs_b, tm)
        y_r = _hgrn_call(x2, w_r, l, lbs[l][None], hgrn_norm_w[l][None], batch, seq, rows)
        q0, q1, k0, k1, v0, v1 = _fox_proj_call(x2, w_k, w_qv_t, w_f, l, fb_pad[l][None], batch, seq, rows)
        y_a = _fox_attn_call(q0, q1, k0, k1, v0, v1, batch, seq, tq)
        x2 = _merge_call(x2, y_g, y_r, y_a, w_bgate, w_pg_b, w_pr_b, w_pa_b, w_o_b, l,
                         ln1_g[l][None], ln1_b[l][None], alpha, tm)
        x2 = _ffn_call(x2, w_gate_b, w_up_b, w_down_b, l, ln2_g[l][None], ln2_b[l][None], alpha,
                       tm, D_FF)
    return x2.reshape(batch, seq, D_MODEL)
```

```python
import functools

import numpy as np
import jax
import jax.numpy as jnp
from jax import lax
from jax.experimental import pallas as pl
from jax.experimental.pallas import tpu as pltpu

BF16 = jnp.bfloat16
F32 = jnp.float32

D_MODEL = 1024
G_GROUPS = 4
G_WIDTH = 512
G_CHUNK = 128
R_HEADS = 4
R_DK = 128
R_WIDTH = 512
A_HEADS = 8
A_HEAD_DIM = 64
A_WIDTH = 512
N_BRANCH = 3
D_FF = 2816
LN_EPS = 1e-5
RMS_EPS = 1e-6
NEG_BIG = -1e30
LB_FLOOR = 1e-30

LOG2E = float(np.log2(np.e))
LANES = 128
SUBLANES = 8
HGRN_CHUNK = 128
HGRN_LEVELS = 7
HGRN_SELECTOR_LEVELS = (1, 2)
VMEM_LIMIT_BYTES = 56 * 1024 * 1024


def _dot(a, b):
    return jnp.dot(a, b, preferred_element_type=F32)


def _dot_nt(a, b):
    return lax.dot_general(a, b, (((1,), (1,)), ((), ())), preferred_element_type=F32)


def _sigmoid(z):
    return 1.0 / (1.0 + jnp.exp(-z))


def _log_sigmoid(z):
    return jnp.minimum(z, 0.0) - jnp.log(1.0 + jnp.exp(-jnp.abs(z)))


def _gelu(z):
    return 0.5 * z * (1.0 + lax.erf(z * np.float32(2.0 ** -0.5)))


def _layer_norm(r, g, b):
    mu = jnp.mean(r, axis=-1, keepdims=True)
    d = r - mu
    var = jnp.mean(d * d, axis=-1, keepdims=True)
    return d * lax.rsqrt(var + LN_EPS) * g + b


def _split_bf16(a, terms):
    parts = []
    rem = a
    for _ in range(terms):
        p = rem.astype(BF16)
        parts.append(p)
        rem = rem - p.astype(F32)
    return parts


def _layer_weight(layer, rows, cols, grid_rank):
    index_map = (lambda i: (layer, 0, 0)) if grid_rank == 1 else (lambda b, r: (layer, 0, 0))
    return pl.BlockSpec((None, rows, cols), index_map, pipeline_mode=pl.Buffered(1))


def _params(sem):
    return pltpu.CompilerParams(dimension_semantics=sem, vmem_limit_bytes=VMEM_LIMIT_BYTES)


def _gmlp_kernel(x_ref, w_ref, lng_ref, lnb_ref, ws_ref, bs_ref, o_ref):
    tm = x_ref.shape[0]
    h = _dot(x_ref[...].astype(BF16), w_ref[...])
    u = _gelu(h[:, :G_WIDTH])
    v = _layer_norm(_gelu(h[:, G_WIDTH:]), lng_ref[...], lnb_ref[...]).astype(BF16)
    row = lax.broadcasted_iota(jnp.int32, (G_CHUNK, G_CHUNK), 0)
    col = lax.broadcasted_iota(jnp.int32, (G_CHUNK, G_CHUNK), 1)
    for g in range(G_GROUPS):
        w_causal = jnp.where(row >= col, ws_ref[g], 0.0).astype(BF16)
        cs = slice(g * LANES, (g + 1) * LANES)
        for n in range(tm // G_CHUNK):
            rs = slice(n * G_CHUNK, (n + 1) * G_CHUNK)
            sv = _dot(w_causal, v[rs, cs]) + bs_ref[g]
            o_ref[rs, cs] = (u[rs, cs] * sv).astype(BF16)


def _gmlp_call(x2, w_uv, layer, ln_g, ln_b, ws, bs_b, tm):
    T = x2.shape[0]
    return pl.pallas_call(
        _gmlp_kernel,
        grid=(T // tm,),
        in_specs=[
            pl.BlockSpec((tm, D_MODEL), lambda i: (i, 0)),
            _layer_weight(layer, D_MODEL, 2 * G_WIDTH, 1),
            pl.BlockSpec((1, G_WIDTH), lambda i: (0, 0)),
            pl.BlockSpec((1, G_WIDTH), lambda i: (0, 0)),
            pl.BlockSpec((G_GROUPS, G_CHUNK, G_CHUNK), lambda i: (0, 0, 0)),
            pl.BlockSpec((G_GROUPS, G_CHUNK, LANES), lambda i: (0, 0, 0)),
        ],
        out_specs=pl.BlockSpec((tm, G_WIDTH), lambda i: (i, 0)),
        out_shape=jax.ShapeDtypeStruct((T, G_WIDTH), BF16),
        compiler_params=_params(("parallel",)),
        name="gmlp",
    )(x2, w_uv, ln_g, ln_b, ws, bs_b)


def _hgrn_constants():
    c = HGRN_CHUNK
    t = np.arange(c)[:, None]
    tau = np.arange(c)[None, :]
    blocks = []
    for level in HGRN_SELECTOR_LEVELS:
        width = 1 << level
        start = (t >> level) * width
        end = start + width - 1
        odd = ((t >> level) & 1) == 1
        blocks.append(np.where(odd, (tau >= start) & (tau <= t), (tau > t) & (tau <= end)))
    blocks.append(tau <= t)
    selectors = np.concatenate(blocks, axis=0).astype(np.float32)
    xor = np.maximum(t ^ tau, 1)
    level_map = np.where(tau < t, np.floor(np.log2(xor)).astype(np.int32),
                         np.where(tau == t, -1, -2)).astype(np.int32)
    return selectors, level_map


def _hgrn_kernel(x_ref, w_ref, lb_ref, nw_ref, sel_ref, lev_ref, o_ref, proj_ref, state_ref):
    rows = x_ref.shape[0]
    c = HGRN_CHUNK

    @pl.when(pl.program_id(1) == 0)
    def _():
        state_ref[...] = jnp.zeros_like(state_ref)

    proj_ref[...] = _dot(x_ref[...].astype(BF16), w_ref[...])
    lb = lb_ref[...]
    log_lb = jnp.log(jnp.maximum(lb, LB_FLOOR))
    log_1m_lb = jnp.log1p(-lb)
    one_m_lb = 1.0 - lb
    norm_w = nw_ref[...]

    def odd_blocks(a, width):
        return jnp.concatenate([a[(2 * g + 1) * width:(2 * g + 2) * width] for g in range(c // (2 * width))], axis=0)

    def gates(ci):
        rs = slice(ci * c, (ci + 1) * c)
        z = proj_ref[rs, R_WIDTH:2 * R_WIDTH]
        b = log_1m_lb + _log_sigmoid(z)
        lf = jnp.maximum(log_lb, b) + jnp.log(1.0 + jnp.exp(-jnp.abs(log_lb - b)))
        key = one_m_lb * _sigmoid(-z)
        lf = lf * np.float32(LOG2E)
        lf_hi, lf_lo = _split_bf16(lf, 2)
        sel = sel_ref[...]
        sums = _dot(sel, lf_hi) + _dot(sel, lf_lo)
        return lf, key, sums

    def chunk(ci, lf, key, sums):
        rs = slice(ci * c, (ci + 1) * c)
        n_sel = len(HGRN_SELECTOR_LEVELS)
        odd_row = (lax.broadcasted_iota(jnp.int32, (c, LANES), 0) & 1) == 1
        lev = lev_ref[...]
        for h in range(R_HEADS):
            hs = slice(h * LANES, (h + 1) * LANES)
            q = proj_ref[rs, h * LANES:(h + 1) * LANES]
            kh = key[:, hs]
            val = proj_ref[rs, 2 * R_WIDTH + h * LANES:2 * R_WIDTH + (h + 1) * LANES]
            gate = proj_ref[rs, 3 * R_WIDTH + h * LANES:3 * R_WIDTH + (h + 1) * LANES]
            q_b = q.astype(BF16)
            k_b = kh.astype(BF16)
            cum = sums[n_sel * c:(n_sel + 1) * c, hs]

            def level_decay(level, cum=cum, hs=hs):
                if level == 0:
                    return jnp.where(odd_row, lf[:, hs], 0.0)
                if level in HGRN_SELECTOR_LEVELS:
                    n = HGRN_SELECTOR_LEVELS.index(level)
                    return sums[n * c:(n + 1) * c, hs]
                width = 1 << level
                pieces = []
                for g in range(c // (2 * width)):
                    base = 2 * g * width
                    boundary = cum[base + width - 1:base + width]
                    pieces.append(boundary - cum[base:base + width])
                    pieces.append(cum[base + width:base + 2 * width] - boundary)
                return jnp.concatenate(pieces, axis=0)

            scores = jnp.zeros((c, c), F32)
            for level in range(HGRN_LEVELS):
                width = 1 << level
                decay = jnp.exp2(level_decay(level))
                decay_b = decay.astype(BF16)
                k_l = k_b * decay_b
                if width < SUBLANES:
                    s_l = _dot_nt(q_b * decay_b, k_l)
                    scores = jnp.where(lev == level, s_l, scores)
                else:
                    q_l = (odd_blocks(q, width) * odd_blocks(decay, width)).astype(BF16)
                    s_l = _dot_nt(q_l, k_l)
                    rows_out = []
                    for g in range(c // (2 * width)):
                        base = 2 * g * width
                        odd = slice(base + width, base + 2 * width)
                        rows_out.append(scores[base:base + width])
                        rows_out.append(jnp.where(lev[odd] == level, s_l[g * width:(g + 1) * width], scores[odd]))
                    scores = jnp.concatenate(rows_out, axis=0)
            diag = jnp.sum(q * kh, axis=-1, keepdims=True)
            scores = jnp.where(lev == -1, diag, scores)
            val_b = val.astype(BF16)
            intra = _dot(scores.astype(BF16), val_b)
            e_in = jnp.exp2(cum)
            e_out = jnp.exp2(cum[c - 1:c] - cum)
            state = state_ref[h]
            inter = _dot_nt((q * e_in).astype(BF16), state.astype(BF16))
            e_last = e_in[c - 1:c, :]
            state_ref[h] = state * e_last + _dot(val.T.astype(BF16), (kh * e_out).astype(BF16))
            o = inter + intra
            o = o * lax.rsqrt(jnp.mean(o * o, axis=-1, keepdims=True) + RMS_EPS)
            o = o * norm_w * (gate * _sigmoid(gate))
            o_ref[rs, h * LANES:(h + 1) * LANES] = o.astype(BF16)

    for ci in range(rows // c):
        chunk(ci, *gates(ci))


def _hgrn_call(x2, w_r, layer, lb, norm_w, batch, seq, rows):
    T = x2.shape[0]
    nr = seq // rows
    selectors, level_map = _hgrn_constants()
    n_sel = selectors.shape[0]
    return pl.pallas_call(
        _hgrn_kernel,
        grid=(batch, nr),
        in_specs=[
            pl.BlockSpec((rows, D_MODEL), lambda b, r: (b * nr + r, 0)),
            _layer_weight(layer, D_MODEL, 4 * R_WIDTH, 2),
            pl.BlockSpec((1, R_WIDTH), lambda b, r: (0, 0)),
            pl.BlockSpec((1, R_DK), lambda b, r: (0, 0)),
            pl.BlockSpec((n_sel, HGRN_CHUNK), lambda b, r: (0, 0)),
            pl.BlockSpec((HGRN_CHUNK, HGRN_CHUNK), lambda b, r: (0, 0)),
        ],
        out_specs=pl.BlockSpec((rows, R_WIDTH), lambda b, r: (b * nr + r, 0)),
        out_shape=jax.ShapeDtypeStruct((T, R_WIDTH), BF16),
        scratch_shapes=[
            pltpu.VMEM((rows, 4 * R_WIDTH), F32),
            pltpu.VMEM((R_HEADS, R_DK, R_DK), F32),
        ],
        compiler_params=_params(("arbitrary", "arbitrary")),
        name="hgrn",
    )(x2, w_r, lb, norm_w, jnp.asarray(selectors, BF16), jnp.asarray(level_map))


FOX_F_TERMS = 3
UNIT_Q = 512


def _fox_placement():
    place = np.zeros((LANES, 2 * A_WIDTH), np.float32)
    for h in range(A_HEADS):
        pair, odd = divmod(h, 2)
        spare0 = pair * LANES + (0 if odd else A_HEAD_DIM)
        for t in range(FOX_F_TERMS):
            place[t * A_HEADS + h, odd * A_WIDTH + spare0 + t] = 1.0
    return place


def _fox_pack_pieces(a, lane):
    packed = jnp.zeros(a.shape, F32)
    for t, piece in reversed(list(enumerate(_split_bf16(a, FOX_F_TERMS)))):
        packed = jnp.where(lane < (t + 1) * A_HEADS, piece.astype(F32), packed)
    return packed.astype(BF16)


def _fox_proj_kernel(x_ref, wk_ref, wqv_ref, wf_ref, fb_ref, tri_ref, place_ref,
                     q0_ref, q1_ref, k0_ref, k1_ref, v0_ref, v1_ref, carry_ref):
    rows = x_ref.shape[0]

    @pl.when(pl.program_id(1) == 0)
    def _():
        carry_ref[...] = jnp.zeros_like(carry_ref)

    xb = x_ref[...].astype(BF16)

    lane = lax.broadcasted_iota(jnp.int32, (rows, LANES), 1)
    log_f = _log_sigmoid(_dot(xb, wf_ref[...]) + fb_ref[...])
    k = _dot(xb, wk_ref[...])
    part = _dot(tri_ref[...], _fox_pack_pieces(log_f, lane))
    q_t = _dot_nt(wqv_ref[:A_WIDTH, :], xb) * np.float32(A_HEAD_DIM ** -0.5 * LOG2E)
    cum = part
    for t in range(1, FOX_F_TERMS):
        cum = cum + pltpu.roll(part, t * A_HEADS, axis=1) + pltpu.roll(part, LANES - t * A_HEADS, axis=1)
    cum = cum + carry_ref[...]
    carry_ref[...] = cum[rows - 1:rows, :]
    placed = _dot(_fox_pack_pieces(cum * np.float32(-LOG2E), lane), place_ref[...])
    v_t = _dot_nt(wqv_ref[A_WIDTH:, :], xb)

    even_lane = (lax.broadcasted_iota(jnp.int32, (rows, A_WIDTH), 1) & A_HEAD_DIM) == 0
    k0_ref[...] = jnp.where(even_lane, k, placed[:, :A_WIDTH]).astype(BF16)
    k1_ref[...] = jnp.where(even_lane, placed[:, A_WIDTH:], k).astype(BF16)

    feature = lax.broadcasted_iota(jnp.int32, (A_WIDTH, rows), 0)
    even_row = (feature & A_HEAD_DIM) == 0
    slot = feature & (A_HEAD_DIM - 1)
    f_ones = jnp.where(slot < FOX_F_TERMS, 1.0, 0.0)
    q0_ref[...] = jnp.where(even_row, q_t, f_ones).astype(BF16)
    q1_ref[...] = jnp.where(even_row, f_ones, q_t).astype(BF16)
    v0_ref[...] = jnp.where(even_row, v_t, 1.0).astype(BF16)
    v1_ref[...] = jnp.where(even_row, 1.0, v_t).astype(BF16)


def _fox_proj_call(x2, w_k, w_qv_t, w_f, layer, fb, batch, seq, rows):
    T = x2.shape[0]
    nr = seq // rows
    tri = jnp.asarray(np.tril(np.ones((rows, rows), np.float32)), BF16)
    place = jnp.asarray(_fox_placement(), BF16)
    row_major = jax.ShapeDtypeStruct((T, A_WIDTH), BF16)
    transposed = jax.ShapeDtypeStruct((batch, A_WIDTH, seq), BF16)
    row_block = lambda width: pl.BlockSpec((rows, width), lambda b, r: (b * nr + r, 0))
    col_block = pl.BlockSpec((None, A_WIDTH, rows), lambda b, r: (b, 0, r))
    return pl.pallas_call(
        _fox_proj_kernel,
        grid=(batch, nr),
        in_specs=[
            row_block(D_MODEL),
            _layer_weight(layer, D_MODEL, A_WIDTH, 2),
            _layer_weight(layer, 2 * A_WIDTH, D_MODEL, 2),
            _layer_weight(layer, D_MODEL, LANES, 2),
            pl.BlockSpec((1, LANES), lambda b, r: (0, 0)),
            pl.BlockSpec((rows, rows), lambda b, r: (0, 0)),
            pl.BlockSpec(place.shape, lambda b, r: (0, 0)),
        ],
        out_specs=[col_block, col_block, row_block(A_WIDTH), row_block(A_WIDTH), col_block, col_block],
        out_shape=[transposed, transposed, row_major, row_major, transposed, transposed],
        scratch_shapes=[pltpu.VMEM((1, LANES), F32)],
        compiler_params=_params(("arbitrary", "arbitrary")),
        name="fox_proj",
    )(x2, w_k, w_qv_t, w_f, fb, tri, place)


def _fox_attn_kernel(q0_ref, q1_ref, k0_ref, k1_ref, v0_ref, v1_ref, o_ref,
                     x_ref, xmax_ref, m_ref, acc_ref, *, tile):
    seq = o_ref.shape[0]
    nq = seq // tile
    total = nq * (nq + 1) // 2
    key_pos = lax.broadcasted_iota(jnp.int32, (tile, UNIT_Q), 0)
    qry_pos = lax.broadcasted_iota(jnp.int32, (tile, UNIT_Q), 1)
    q_refs = (q0_ref, q1_ref)
    k_refs = (k0_ref, k1_ref)
    v_refs = (v0_ref, v1_ref)

    n_sub = tile // UNIT_Q
    units = [(h, u) for h in range(2) for u in range(n_sub)]

    def scores(i, j, slot, masked, unit):
        h, u = unit
        qc = pl.multiple_of(i * tile + u * UNIT_Q, UNIT_Q)
        kc = pl.multiple_of(j * tile, tile)
        x = _dot(k_refs[h][pl.ds(kc, tile), :], q_refs[h][:, pl.ds(qc, UNIT_Q)])
        if masked:
            x = jnp.where(key_pos <= qry_pos + u * UNIT_Q, x, NEG_BIG)
        x_ref[slot, h, u] = x
        xmax_ref[slot, h, u] = jnp.max(x, axis=0, keepdims=True)

    def accumulate(j, slot, unit):
        h, u = unit
        kc = pl.multiple_of(j * tile, tile)
        m = m_ref[h, u]
        m_new = jnp.maximum(m, xmax_ref[slot, h, u])
        p = jnp.exp2((x_ref[slot, h, u] - m_new).astype(BF16))
        acc_ref[h, u] = jnp.exp2(m - m_new) * acc_ref[h, u] + _dot(v_refs[h][:, pl.ds(kc, tile)], p)
        m_ref[h, u] = m_new

    def reset():
        m_ref[...] = jnp.full(m_ref.shape, NEG_BIG, F32)
        acc_ref[...] = jnp.zeros(acc_ref.shape, F32)

    def finalize(i):
        qc = pl.multiple_of(i * tile, tile)
        a0 = jnp.concatenate([acc_ref[0, u] for u in range(n_sub)], axis=1)
        a1 = jnp.concatenate([acc_ref[1, u] for u in range(n_sub)], axis=1)
        out_t = jnp.concatenate([a0[:A_HEAD_DIM] / a0[A_HEAD_DIM:A_HEAD_DIM + 1],
                                 a1[A_HEAD_DIM:] / a1[0:1]], axis=0)
        o_ref[pl.ds(qc, tile), :] = out_t.T.astype(BF16)
        reset()

    def advance(i, j):
        wrap = j >= i
        return jnp.where(wrap, i + 1, i), jnp.where(wrap, 0, j + 1)

    def half(t, i, j, slot):
        ni, nj = advance(i, j)
        ci = jnp.minimum(ni, nq - 1)
        cj = jnp.minimum(nj, ci)
        valid = t < total

        for next_diagonal in (False, True):
            @pl.when(valid & ((cj == ci) == next_diagonal))
            def _(next_diagonal=next_diagonal):
                for unit in units:
                    scores(ci, cj, 1 - slot, next_diagonal, unit)
                    accumulate(j, slot, unit)

        @pl.when(valid & (j == i))
        def _():
            finalize(i)

        return ni, nj

    def body(u, carry):
        i, j = carry
        i, j = half(2 * u, i, j, 0)
        return half(2 * u + 1, i, j, 1)

    reset()
    for unit in units:
        scores(0, 0, 0, True, unit)
    lax.fori_loop(0, (total + 1) // 2, body, (jnp.int32(0), jnp.int32(0)))


def _fox_attn_call(q0, q1, k0, k1, v0, v1, batch, seq, tile):
    T = batch * seq
    pairs = A_HEADS // 2
    row_major = pl.BlockSpec((seq, LANES), lambda b, p: (b, p))
    transposed = pl.BlockSpec((None, LANES, seq), lambda b, p: (b, p, 0))
    n_sub = tile // UNIT_Q
    return pl.pallas_call(
        functools.partial(_fox_attn_kernel, tile=tile),
        grid=(batch, pairs),
        in_specs=[transposed, transposed, row_major, row_major, transposed, transposed],
        out_specs=row_major,
        out_shape=jax.ShapeDtypeStruct((T, A_WIDTH), BF16),
        scratch_shapes=[
            pltpu.VMEM((2, 2, n_sub, tile, UNIT_Q), F32),
            pltpu.VMEM((2, 2, n_sub, 1, UNIT_Q), F32),
            pltpu.VMEM((2, n_sub, 1, UNIT_Q), F32),
            pltpu.VMEM((2, n_sub, LANES, UNIT_Q), F32),
        ],
        compiler_params=_params(("parallel", "parallel")),
        name="fox_attn",
    )(q0, q1, k0, k1, v0, v1)


def _merge_kernel(x_ref, yg_ref, yr_ref, ya_ref, wgate_ref, wpg_ref, wpr_ref, wpa_ref, wo_ref,
                  g_ref, b_ref, o_ref, *, alpha):
    half = x_ref.shape[0] // 2
    for s in range(2):
        rs = slice(s * half, (s + 1) * half)
        x = x_ref[rs, :]
        xb = x.astype(BF16)
        mixed = None
        for n, (y_ref, wp_ref) in enumerate(((yg_ref, wpg_ref), (yr_ref, wpr_ref), (ya_ref, wpa_ref))):
            gate = _sigmoid(_dot(xb, wgate_ref[:, n * D_MODEL:(n + 1) * D_MODEL]))
            term = gate * _dot(y_ref[rs, :], wp_ref[...])
            mixed = term if mixed is None else mixed + term
        out = _dot(mixed.astype(BF16), wo_ref[...])
        o_ref[rs, :] = _layer_norm(alpha * x + out, g_ref[...], b_ref[...])


def _merge_call(x2, yg, yr, ya, w_gate, w_pg, w_pr, w_pa, w_o, layer, ln_g, ln_b, alpha, tm):
    T = x2.shape[0]
    rows = lambda width: pl.BlockSpec((tm, width), lambda i: (i, 0))
    weight = lambda r, c: _layer_weight(layer, r, c, 1)
    vec = pl.BlockSpec((1, D_MODEL), lambda i: (0, 0))
    return pl.pallas_call(
        functools.partial(_merge_kernel, alpha=alpha),
        grid=(T // tm,),
        in_specs=[
            rows(D_MODEL), rows(G_WIDTH), rows(R_WIDTH), rows(A_WIDTH),
            weight(D_MODEL, N_BRANCH * D_MODEL),
            weight(G_WIDTH, D_MODEL), weight(R_WIDTH, D_MODEL), weight(A_WIDTH, D_MODEL),
            weight(D_MODEL, D_MODEL), vec, vec,
        ],
        out_specs=rows(D_MODEL),
        out_shape=jax.ShapeDtypeStruct((T, D_MODEL), F32),
        compiler_params=_params(("parallel",)),
        name="merge",
    )(x2, yg, yr, ya, w_gate, w_pg, w_pr, w_pa, w_o, ln_g, ln_b)


def _ffn_kernel(x_ref, wg_ref, wu_ref, wd_ref, g_ref, b_ref, o_ref, *, alpha):
    half = x_ref.shape[0] // 2
    for s in range(2):
        rs = slice(s * half, (s + 1) * half)
        x = x_ref[rs, :]
        xb = x.astype(BF16)
        hg = _dot(xb, wg_ref[...])
        hu = _dot(xb, wu_ref[...])
        hid = (hg * _sigmoid(hg) * hu).astype(BF16)
        out = _dot(hid, wd_ref[...])
        o_ref[rs, :] = _layer_norm(alpha * x + out, g_ref[...], b_ref[...])


def _ffn_call(x2, w_gate, w_up, w_down, layer, ln_g, ln_b, alpha, tm):
    T = x2.shape[0]
    rows = pl.BlockSpec((tm, D_MODEL), lambda i: (i, 0))
    vec = pl.BlockSpec((1, D_MODEL), lambda i: (0, 0))
    return pl.pallas_call(
        functools.partial(_ffn_kernel, alpha=alpha),
        grid=(T // tm,),
        in_specs=[
            rows,
            _layer_weight(layer, D_MODEL, D_FF, 1),
            _layer_weight(layer, D_MODEL, D_FF, 1),
            _layer_weight(layer, D_FF, D_MODEL, 1),
            vec, vec,
        ],
        out_specs=rows,
        out_shape=jax.ShapeDtypeStruct((T, D_MODEL), F32),
        compiler_params=_params(("parallel",)),
        name="ffn",
    )(x2, w_gate, w_up, w_down, ln_g, ln_b)


def kernel(x, w_in, gmlp_ln_g, gmlp_ln_b, gmlp_ws, gmlp_bs, hgrn_lb_logits, hgrn_norm_w, fox_fb,
           w_pg, w_pr, w_pa, w_o, ln1_g, ln1_b, w_gate, w_up, w_down, ln2_g, ln2_b):
    batch, seq, _ = x.shape
    depth = w_in.shape[0]
    T = batch * seq
    alpha = float((2 * depth) ** 0.25)
    rows = min(512, seq)
    tq = min(512, seq)
    tm = 512

    o_r = 2 * G_WIDTH
    o_a = o_r + 4 * R_WIDTH
    o_f = o_a + 3 * A_WIDTH
    o_gate = o_f + A_HEADS
    w_uv = w_in[:, :, :o_r].astype(BF16)
    w_r = w_in[:, :, o_r:o_a].astype(BF16)
    w_k = w_in[:, :, o_a + A_WIDTH:o_a + 2 * A_WIDTH].astype(BF16)
    w_qv_t = jnp.concatenate([w_in[:, :, o_a:o_a + A_WIDTH], w_in[:, :, o_a + 2 * A_WIDTH:o_f]],
                             axis=2).transpose(0, 2, 1).astype(BF16)
    spare = LANES - FOX_F_TERMS * A_HEADS
    w_f = jnp.pad(jnp.tile(w_in[:, :, o_f:o_gate], (1, 1, FOX_F_TERMS)), ((0, 0), (0, 0), (0, spare))).astype(BF16)
    w_bgate = w_in[:, :, o_gate:].astype(BF16)
    w_pg_b, w_pr_b, w_pa_b, w_o_b = (w.astype(BF16) for w in (w_pg, w_pr, w_pa, w_o))
    w_gate_b, w_up_b, w_down_b = (w.astype(BF16) for w in (w_gate, w_up, w_down))

    probs = jax.nn.softmax(hgrn_lb_logits.astype(F32), axis=0)
    lbs = jnp.cumsum(probs, axis=0) - probs[0:1]
    fb_pad = jnp.pad(jnp.tile(fox_fb.astype(F32), (1, FOX_F_TERMS)), ((0, 0), (0, spare)))

    x2 = x.reshape(T, D_MODEL)
    for l in range(depth):
        bs_b = jnp.broadcast_to(gmlp_bs[l][:, :, None], (G_GROUPS, G_CHUNK, LANES))
        y_g = _gmlp_call(x2, w_uv, l, gmlp_ln_g[l][None], gmlp_ln_b[l][None], gmlp_ws[l], bs_b, 2 * tm)
        y_r = _hgrn_call(x2, w_r, l, lbs[l][None], hgrn_norm_w[l][None], batch, seq, rows)
        q0, q1, k0, k1, v0, v1 = _fox_proj_call(x2, w_k, w_qv_t, w_f, l, fb_pad[l][None], batch, seq, rows)
        y_a = _fox_attn_call(q0, q1, k0, k1, v0, v1, batch, seq, tq)
        x2 = _merge_call(x2, y_g, y_r, y_a, w_bgate, w_pg_b, w_pr_b, w_pa_b, w_o_b, l,
                         ln1_g[l][None], ln1_b[l][None], alpha, tm)
        x2 = _ffn_call(x2, w_gate_b, w_up_b, w_down_b, l, ln2_g[l][None], ln2_b[l][None], alpha, tm)
    return x2.reshape(batch, seq, D_MODEL)
```

```python
import functools

import numpy as np
import jax
import jax.numpy as jnp
from jax import lax
from jax.experimental import pallas as pl
from jax.experimental.pallas import tpu as pltpu

BF16 = jnp.bfloat16
F32 = jnp.float32

D_MODEL = 1024
G_GROUPS = 4
G_WIDTH = 512
G_CHUNK = 128
R_HEADS = 4
R_DK = 128
R_WIDTH = 512
A_HEADS = 8
A_HEAD_DIM = 64
A_WIDTH = 512
N_BRANCH = 3
D_FF = 2816
LN_EPS = 1e-5
RMS_EPS = 1e-6
NEG_BIG = -1e30
LB_FLOOR = 1e-30

LOG2E = float(np.log2(np.e))
LANES = 128
SUBLANES = 8
HGRN_CHUNK = 128
HGRN_LEVELS = 7
HGRN_SELECTOR_LEVELS = (1, 2)
VMEM_LIMIT_BYTES = 56 * 1024 * 1024


def _dot(a, b):
    return jnp.dot(a, b, preferred_element_type=F32)


def _dot_nt(a, b):
    return lax.dot_general(a, b, (((1,), (1,)), ((), ())), preferred_element_type=F32)


def _sigmoid(z):
    return 1.0 / (1.0 + jnp.exp(-z))


def _log_sigmoid(z):
    return jnp.minimum(z, 0.0) - jnp.log(1.0 + jnp.exp(-jnp.abs(z)))


def _gelu(z):
    return 0.5 * z * (1.0 + lax.erf(z * np.float32(2.0 ** -0.5)))


def _layer_norm(r, g, b):
    mu = jnp.mean(r, axis=-1, keepdims=True)
    d = r - mu
    var = jnp.mean(d * d, axis=-1, keepdims=True)
    return d * lax.rsqrt(var + LN_EPS) * g + b


def _split_bf16(a, terms):
    parts = []
    rem = a
    for _ in range(terms):
        p = rem.astype(BF16)
        parts.append(p)
        rem = rem - p.astype(F32)
    return parts


def _layer_weight(layer, rows, cols, grid_rank):
    index_map = (lambda i: (layer, 0, 0)) if grid_rank == 1 else (lambda b, r: (layer, 0, 0))
    return pl.BlockSpec((None, rows, cols), index_map, pipeline_mode=pl.Buffered(1))


def _params(sem):
    return pltpu.CompilerParams(dimension_semantics=sem, vmem_limit_bytes=VMEM_LIMIT_BYTES)


def _gmlp_kernel(x_ref, w_ref, lng_ref, lnb_ref, ws_ref, bs_ref, o_ref):
    tm = x_ref.shape[0]
    h = _dot(x_ref[...].astype(BF16), w_ref[...])
    u = _gelu(h[:, :G_WIDTH])
    v = _layer_norm(_gelu(h[:, G_WIDTH:]), lng_ref[...], lnb_ref[...]).astype(BF16)
    row = lax.broadcasted_iota(jnp.int32, (G_CHUNK, G_CHUNK), 0)
    col = lax.broadcasted_iota(jnp.int32, (G_CHUNK, G_CHUNK), 1)
    for g in range(G_GROUPS):
        w_causal = jnp.where(row >= col, ws_ref[g], 0.0).astype(BF16)
        cs = slice(g * LANES, (g + 1) * LANES)
        for n in range(tm // G_CHUNK):
            rs = slice(n * G_CHUNK, (n + 1) * G_CHUNK)
            sv = _dot(w_causal, v[rs, cs]) + bs_ref[g]
            o_ref[rs, cs] = (u[rs, cs] * sv).astype(BF16)


def _gmlp_call(x2, w_uv, layer, ln_g, ln_b, ws, bs_b, tm):
    T = x2.shape[0]
    return pl.pallas_call(
        _gmlp_kernel,
        grid=(T // tm,),
        in_specs=[
            pl.BlockSpec((tm, D_MODEL), lambda i: (i, 0)),
            _layer_weight(layer, D_MODEL, 2 * G_WIDTH, 1),
            pl.BlockSpec((1, G_WIDTH), lambda i: (0, 0)),
            pl.BlockSpec((1, G_WIDTH), lambda i: (0, 0)),
            pl.BlockSpec((G_GROUPS, G_CHUNK, G_CHUNK), lambda i: (0, 0, 0)),
            pl.BlockSpec((G_GROUPS, G_CHUNK, LANES), lambda i: (0, 0, 0)),
        ],
        out_specs=pl.BlockSpec((tm, G_WIDTH), lambda i: (i, 0)),
        out_shape=jax.ShapeDtypeStruct((T, G_WIDTH), BF16),
        compiler_params=_params(("parallel",)),
        name="gmlp",
    )(x2, w_uv, ln_g, ln_b, ws, bs_b)


def _hgrn_constants():
    c = HGRN_CHUNK
    t = np.arange(c)[:, None]
    tau = np.arange(c)[None, :]
    blocks = []
    for level in HGRN_SELECTOR_LEVELS:
        width = 1 << level
        start = (t >> level) * width
        end = start + width - 1
        odd = ((t >> level) & 1) == 1
        blocks.append(np.where(odd, (tau >= start) & (tau <= t), (tau > t) & (tau <= end)))
    blocks.append(tau <= t)
    selectors = np.concatenate(blocks, axis=0).astype(np.float32)
    xor = np.maximum(t ^ tau, 1)
    level_map = np.where(tau < t, np.floor(np.log2(xor)).astype(np.int32),
                         np.where(tau == t, -1, -2)).astype(np.int32)
    return selectors, level_map


def _hgrn_kernel(x_ref, w_ref, lb_ref, nw_ref, sel_ref, lev_ref, o_ref, proj_ref, state_ref):
    rows = x_ref.shape[0]
    c = HGRN_CHUNK

    @pl.when(pl.program_id(1) == 0)
    def _():
        state_ref[...] = jnp.zeros_like(state_ref)

    proj_ref[...] = _dot(x_ref[...].astype(BF16), w_ref[...])
    lb = lb_ref[...]
    log_lb = jnp.log(jnp.maximum(lb, LB_FLOOR))
    log_1m_lb = jnp.log1p(-lb)
    one_m_lb = 1.0 - lb
    norm_w = nw_ref[...]

    def odd_blocks(a, width):
        return jnp.concatenate([a[(2 * g + 1) * width:(2 * g + 2) * width] for g in range(c // (2 * width))], axis=0)

    def gates(ci):
        rs = slice(ci * c, (ci + 1) * c)
        z = proj_ref[rs, R_WIDTH:2 * R_WIDTH]
        b = log_1m_lb + _log_sigmoid(z)
        lf = jnp.maximum(log_lb, b) + jnp.log(1.0 + jnp.exp(-jnp.abs(log_lb - b)))
        key = one_m_lb * _sigmoid(-z)
        lf = lf * np.float32(LOG2E)
        lf_hi, lf_lo = _split_bf16(lf, 2)
        sel = sel_ref[...]
        sums = _dot(sel, lf_hi) + _dot(sel, lf_lo)
        return lf, key, sums

    def chunk(ci, lf, key, sums):
        rs = slice(ci * c, (ci + 1) * c)
        n_sel = len(HGRN_SELECTOR_LEVELS)
        odd_row = (lax.broadcasted_iota(jnp.int32, (c, LANES), 0) & 1) == 1
        lev = lev_ref[...]
        for h in range(R_HEADS):
            hs = slice(h * LANES, (h + 1) * LANES)
            q = proj_ref[rs, h * LANES:(h + 1) * LANES]
            kh = key[:, hs]
            val = proj_ref[rs, 2 * R_WIDTH + h * LANES:2 * R_WIDTH + (h + 1) * LANES]
            gate = proj_ref[rs, 3 * R_WIDTH + h * LANES:3 * R_WIDTH + (h + 1) * LANES]
            q_b = q.astype(BF16)
            k_b = kh.astype(BF16)
            cum = sums[n_sel * c:(n_sel + 1) * c, hs]

            def level_decay(level, cum=cum, hs=hs):
                if level == 0:
                    return jnp.where(odd_row, lf[:, hs], 0.0)
                if level in HGRN_SELECTOR_LEVELS:
                    n = HGRN_SELECTOR_LEVELS.index(level)
                    return sums[n * c:(n + 1) * c, hs]
                width = 1 << level
                pieces = []
                for g in range(c // (2 * width)):
                    base = 2 * g * width
                    boundary = cum[base + width - 1:base + width]
                    pieces.append(boundary - cum[base:base + width])
                    pieces.append(cum[base + width:base + 2 * width] - boundary)
                return jnp.concatenate(pieces, axis=0)

            scores = jnp.zeros((c, c), F32)
            for level in range(HGRN_LEVELS):
                width = 1 << level
                decay = jnp.exp2(level_decay(level))
                decay_b = decay.astype(BF16)
                k_l = k_b * decay_b
                if width < SUBLANES:
                    s_l = _dot_nt(q_b * decay_b, k_l)
                    scores = jnp.where(lev == level, s_l, scores)
                else:
                    q_l = (odd_blocks(q, width) * odd_blocks(decay, width)).astype(BF16)
                    s_l = _dot_nt(q_l, k_l)
                    rows_out = []
                    for g in range(c // (2 * width)):
                        base = 2 * g * width
                        odd = slice(base + width, base + 2 * width)
                        rows_out.append(scores[base:base + width])
                        rows_out.append(jnp.where(lev[odd] == level, s_l[g * width:(g + 1) * width], scores[odd]))
                    scores = jnp.concatenate(rows_out, axis=0)
            diag = jnp.sum(q * kh, axis=-1, keepdims=True)
            scores = jnp.where(lev == -1, diag, scores)
            val_b = val.astype(BF16)
            intra = _dot(scores.astype(BF16), val_b)
            e_in = jnp.exp2(cum)
            e_out = jnp.exp2(cum[c - 1:c] - cum)
            state = state_ref[h]
            inter = _dot_nt((q * e_in).astype(BF16), state.astype(BF16))
            e_last = e_in[c - 1:c, :]
            state_ref[h] = state * e_last + _dot(val.T.astype(BF16), (kh * e_out).astype(BF16))
            o = inter + intra
            o = o * lax.rsqrt(jnp.mean(o * o, axis=-1, keepdims=True) + RMS_EPS)
            o = o * norm_w * (gate * _sigmoid(gate))
            o_ref[rs, h * LANES:(h + 1) * LANES] = o.astype(BF16)

    for ci in range(rows // c):
        chunk(ci, *gates(ci))


def _hgrn_call(x2, w_r, layer, lb, norm_w, batch, seq, rows):
    T = x2.shape[0]
    nr = seq // rows
    selectors, level_map = _hgrn_constants()
    n_sel = selectors.shape[0]
    return pl.pallas_call(
        _hgrn_kernel,
        grid=(batch, nr),
        in_specs=[
            pl.BlockSpec((rows, D_MODEL), lambda b, r: (b * nr + r, 0)),
            _layer_weight(layer, D_MODEL, 4 * R_WIDTH, 2),
            pl.BlockSpec((1, R_WIDTH), lambda b, r: (0, 0)),
            pl.BlockSpec((1, R_DK), lambda b, r: (0, 0)),
            pl.BlockSpec((n_sel, HGRN_CHUNK), lambda b, r: (0, 0)),
            pl.BlockSpec((HGRN_CHUNK, HGRN_CHUNK), lambda b, r: (0, 0)),
        ],
        out_specs=pl.BlockSpec((rows, R_WIDTH), lambda b, r: (b * nr + r, 0)),
        out_shape=jax.ShapeDtypeStruct((T, R_WIDTH), BF16),
        scratch_shapes=[
            pltpu.VMEM((rows, 4 * R_WIDTH), F32),
            pltpu.VMEM((R_HEADS, R_DK, R_DK), F32),
        ],
        compiler_params=_params(("arbitrary", "arbitrary")),
        name="hgrn",
    )(x2, w_r, lb, norm_w, jnp.asarray(selectors, BF16), jnp.asarray(level_map))


FOX_F_TERMS = 3
FOX_PAIRS_PER_STEP = 2
UNIT_Q = 512


def _fox_placement():
    place = np.zeros((LANES, 2 * A_WIDTH), np.float32)
    for h in range(A_HEADS):
        pair, odd = divmod(h, 2)
        spare0 = pair * LANES + (0 if odd else A_HEAD_DIM)
        for t in range(FOX_F_TERMS):
            place[t * A_HEADS + h, odd * A_WIDTH + spare0 + t] = 1.0
    return place


def _fox_pack_pieces(a, lane):
    packed = jnp.zeros(a.shape, F32)
    for t, piece in reversed(list(enumerate(_split_bf16(a, FOX_F_TERMS)))):
        packed = jnp.where(lane < (t + 1) * A_HEADS, piece.astype(F32), packed)
    return packed.astype(BF16)


def _fox_proj_kernel(x_ref, wk_ref, wqv_ref, wf_ref, fb_ref, tri_ref, place_ref,
                     q0_ref, q1_ref, k0_ref, k1_ref, v0_ref, v1_ref, carry_ref):
    rows = x_ref.shape[0]

    @pl.when(pl.program_id(1) == 0)
    def _():
        carry_ref[...] = jnp.zeros_like(carry_ref)

    xb = x_ref[...].astype(BF16)

    lane = lax.broadcasted_iota(jnp.int32, (rows, LANES), 1)
    log_f = _log_sigmoid(_dot(xb, wf_ref[...]) + fb_ref[...])
    k = _dot(xb, wk_ref[...])
    part = _dot(tri_ref[...], _fox_pack_pieces(log_f, lane))
    q_t = _dot_nt(wqv_ref[:A_WIDTH, :], xb) * np.float32(A_HEAD_DIM ** -0.5 * LOG2E)
    cum = part
    for t in range(1, FOX_F_TERMS):
        cum = cum + pltpu.roll(part, t * A_HEADS, axis=1) + pltpu.roll(part, LANES - t * A_HEADS, axis=1)
    cum = cum + carry_ref[...]
    carry_ref[...] = cum[rows - 1:rows, :]
    placed = _dot(_fox_pack_pieces(cum * np.float32(-LOG2E), lane), place_ref[...])
    v_t = _dot_nt(wqv_ref[A_WIDTH:, :], xb)

    even_lane = (lax.broadcasted_iota(jnp.int32, (rows, A_WIDTH), 1) & A_HEAD_DIM) == 0
    k0_ref[...] = jnp.where(even_lane, k, placed[:, :A_WIDTH]).astype(BF16)
    k1_ref[...] = jnp.where(even_lane, placed[:, A_WIDTH:], k).astype(BF16)

    feature = lax.broadcasted_iota(jnp.int32, (A_WIDTH, rows), 0)
    even_row = (feature & A_HEAD_DIM) == 0
    slot = feature & (A_HEAD_DIM - 1)
    f_ones = jnp.where(slot < FOX_F_TERMS, 1.0, 0.0)
    q0_ref[...] = jnp.where(even_row, q_t, f_ones).astype(BF16)
    q1_ref[...] = jnp.where(even_row, f_ones, q_t).astype(BF16)
    v0_ref[...] = jnp.where(even_row, v_t, 1.0).astype(BF16)
    v1_ref[...] = jnp.where(even_row, 1.0, v_t).astype(BF16)


def _fox_proj_call(x2, w_k, w_qv_t, w_f, layer, fb, batch, seq, rows):
    T = x2.shape[0]
    nr = seq // rows
    tri = jnp.asarray(np.tril(np.ones((rows, rows), np.float32)), BF16)
    place = jnp.asarray(_fox_placement(), BF16)
    row_major = jax.ShapeDtypeStruct((T, A_WIDTH), BF16)
    transposed = jax.ShapeDtypeStruct((batch, A_WIDTH, seq), BF16)
    row_block = lambda width: pl.BlockSpec((rows, width), lambda b, r: (b * nr + r, 0))
    col_block = pl.BlockSpec((None, A_WIDTH, rows), lambda b, r: (b, 0, r))
    return pl.pallas_call(
        _fox_proj_kernel,
        grid=(batch, nr),
        in_specs=[
            row_block(D_MODEL),
            _layer_weight(layer, D_MODEL, A_WIDTH, 2),
            _layer_weight(layer, 2 * A_WIDTH, D_MODEL, 2),
            _layer_weight(layer, D_MODEL, LANES, 2),
            pl.BlockSpec((1, LANES), lambda b, r: (0, 0)),
            pl.BlockSpec((rows, rows), lambda b, r: (0, 0)),
            pl.BlockSpec(place.shape, lambda b, r: (0, 0)),
        ],
        out_specs=[col_block, col_block, row_block(A_WIDTH), row_block(A_WIDTH), col_block, col_block],
        out_shape=[transposed, transposed, row_major, row_major, transposed, transposed],
        scratch_shapes=[pltpu.VMEM((1, LANES), F32)],
        compiler_params=_params(("arbitrary", "arbitrary")),
        name="fox_proj",
    )(x2, w_k, w_qv_t, w_f, fb, tri, place)


def _fox_attn_kernel(q0_ref, q1_ref, k0_ref, k1_ref, v0_ref, v1_ref, o_ref,
                     x_ref, xmax_ref, m_ref, acc_ref, *, tile):
    seq = o_ref.shape[0]
    nq = seq // tile
    total = nq * (nq + 1) // 2
    key_pos = lax.broadcasted_iota(jnp.int32, (tile, UNIT_Q), 0)
    qry_pos = lax.broadcasted_iota(jnp.int32, (tile, UNIT_Q), 1)
    q_refs = (q0_ref, q1_ref)
    k_refs = (k0_ref, k1_ref)
    v_refs = (v0_ref, v1_ref)

    n_sub = tile // UNIT_Q
    n_heads = 2 * FOX_PAIRS_PER_STEP
    units = [(h, u) for h in range(n_heads) for u in range(n_sub)]

    def pair_block(h):
        return slice((h // 2) * LANES, (h // 2 + 1) * LANES)

    def scores(i, j, slot, masked, unit):
        h, u = unit
        qc = pl.multiple_of(i * tile + u * UNIT_Q, UNIT_Q)
        kc = pl.multiple_of(j * tile, tile)
        x = _dot(k_refs[h % 2][pl.ds(kc, tile), pair_block(h)], q_refs[h % 2][pair_block(h), pl.ds(qc, UNIT_Q)])
        if masked:
            x = jnp.where(key_pos <= qry_pos + u * UNIT_Q, x, NEG_BIG)
        x_ref[slot, h, u] = x
        xmax_ref[slot, h, u] = jnp.max(x, axis=0, keepdims=True)

    def accumulate(j, slot, unit):
        h, u = unit
        kc = pl.multiple_of(j * tile, tile)
        m = m_ref[h, u]
        m_new = jnp.maximum(m, xmax_ref[slot, h, u])
        p = jnp.exp2((x_ref[slot, h, u] - m_new).astype(BF16))
        acc_ref[h, u] = (jnp.exp2(m - m_new) * acc_ref[h, u]
                         + _dot(v_refs[h % 2][pair_block(h), pl.ds(kc, tile)], p))
        m_ref[h, u] = m_new

    def reset():
        m_ref[...] = jnp.full(m_ref.shape, NEG_BIG, F32)
        acc_ref[...] = jnp.zeros(acc_ref.shape, F32)

    def finalize(i):
        qc = pl.multiple_of(i * tile, tile)
        for pair in range(FOX_PAIRS_PER_STEP):
            a0 = jnp.concatenate([acc_ref[2 * pair, u] for u in range(n_sub)], axis=1)
            a1 = jnp.concatenate([acc_ref[2 * pair + 1, u] for u in range(n_sub)], axis=1)
            out_t = jnp.concatenate([a0[:A_HEAD_DIM] / a0[A_HEAD_DIM:A_HEAD_DIM + 1],
                                     a1[A_HEAD_DIM:] / a1[0:1]], axis=0)
            o_ref[pl.ds(qc, tile), pair * LANES:(pair + 1) * LANES] = out_t.T.astype(BF16)
        reset()

    def advance(i, j):
        wrap = j >= i
        return jnp.where(wrap, i + 1, i), jnp.where(wrap, 0, j + 1)

    def half(t, i, j, slot):
        ni, nj = advance(i, j)
        ci = jnp.minimum(ni, nq - 1)
        cj = jnp.minimum(nj, ci)
        valid = t < total

        for next_diagonal in (False, True):
            @pl.when(valid & ((cj == ci) == next_diagonal))
            def _(next_diagonal=next_diagonal):
                for unit in units:
                    scores(ci, cj, 1 - slot, next_diagonal, unit)
                    accumulate(j, slot, unit)

        @pl.when(valid & (j == i))
        def _():
            finalize(i)

        return ni, nj

    def body(u, carry):
        i, j = carry
        i, j = half(2 * u, i, j, 0)
        return half(2 * u + 1, i, j, 1)

    reset()
    for unit in units:
        scores(0, 0, 0, True, unit)
    lax.fori_loop(0, (total + 1) // 2, body, (jnp.int32(0), jnp.int32(0)))


def _fox_attn_call(q0, q1, k0, k1, v0, v1, batch, seq, tile):
    T = batch * seq
    steps = A_HEADS // (2 * FOX_PAIRS_PER_STEP)
    width = FOX_PAIRS_PER_STEP * LANES
    n_heads = 2 * FOX_PAIRS_PER_STEP
    row_major = pl.BlockSpec((seq, width), lambda b, p: (b, p))
    transposed = pl.BlockSpec((None, width, seq), lambda b, p: (b, p, 0))
    n_sub = tile // UNIT_Q
    return pl.pallas_call(
        functools.partial(_fox_attn_kernel, tile=tile),
        grid=(batch, steps),
        in_specs=[transposed, transposed, row_major, row_major, transposed, transposed],
        out_specs=row_major,
        out_shape=jax.ShapeDtypeStruct((T, A_WIDTH), BF16),
        scratch_shapes=[
            pltpu.VMEM((2, n_heads, n_sub, tile, UNIT_Q), F32),
            pltpu.VMEM((2, n_heads, n_sub, 1, UNIT_Q), F32),
            pltpu.VMEM((n_heads, n_sub, 1, UNIT_Q), F32),
            pltpu.VMEM((n_heads, n_sub, LANES, UNIT_Q), F32),
        ],
        compiler_params=_params(("parallel", "parallel")),
        name="fox_attn",
    )(q0, q1, k0, k1, v0, v1)


def _merge_kernel(x_ref, yg_ref, yr_ref, ya_ref, wgate_ref, wpg_ref, wpr_ref, wpa_ref, wo_ref,
                  g_ref, b_ref, o_ref, *, alpha):
    half = x_ref.shape[0] // 2
    for s in range(2):
        rs = slice(s * half, (s + 1) * half)
        x = x_ref[rs, :]
        xb = x.astype(BF16)
        mixed = None
        for n, (y_ref, wp_ref) in enumerate(((yg_ref, wpg_ref), (yr_ref, wpr_ref), (ya_ref, wpa_ref))):
            gate = _sigmoid(_dot(xb, wgate_ref[:, n * D_MODEL:(n + 1) * D_MODEL]))
            term = gate * _dot(y_ref[rs, :], wp_ref[...])
            mixed = term if mixed is None else mixed + term
        out = _dot(mixed.astype(BF16), wo_ref[...])
        o_ref[rs, :] = _layer_norm(alpha * x + out, g_ref[...], b_ref[...])


def _merge_call(x2, yg, yr, ya, w_gate, w_pg, w_pr, w_pa, w_o, layer, ln_g, ln_b, alpha, tm):
    T = x2.shape[0]
    rows = lambda width: pl.BlockSpec((tm, width), lambda i: (i, 0))
    weight = lambda r, c: _layer_weight(layer, r, c, 1)
    vec = pl.BlockSpec((1, D_MODEL), lambda i: (0, 0))
    return pl.pallas_call(
        functools.partial(_merge_kernel, alpha=alpha),
        grid=(T // tm,),
        in_specs=[
            rows(D_MODEL), rows(G_WIDTH), rows(R_WIDTH), rows(A_WIDTH),
            weight(D_MODEL, N_BRANCH * D_MODEL),
            weight(G_WIDTH, D_MODEL), weight(R_WIDTH, D_MODEL), weight(A_WIDTH, D_MODEL),
            weight(D_MODEL, D_MODEL), vec, vec,
        ],
        out_specs=rows(D_MODEL),
        out_shape=jax.ShapeDtypeStruct((T, D_MODEL), F32),
        compiler_params=_params(("parallel",)),
        name="merge",
    )(x2, yg, yr, ya, w_gate, w_pg, w_pr, w_pa, w_o, ln_g, ln_b)


def _ffn_kernel(x_ref, wg_ref, wu_ref, wd_ref, g_ref, b_ref, o_ref, *, alpha):
    half = x_ref.shape[0] // 2
    halves = [slice(s * half, (s + 1) * half) for s in range(2)]
    hidden = []
    for rs in halves:
        xb = x_ref[rs, :].astype(BF16)
        hg = _dot(xb, wg_ref[...])
        hu = _dot(xb, wu_ref[...])
        hidden.append((hg * _sigmoid(hg) * hu).astype(BF16))
    for rs, hid in zip(halves, hidden):
        out = _dot(hid, wd_ref[...])
        o_ref[rs, :] = _layer_norm(alpha * x_ref[rs, :] + out, g_ref[...], b_ref[...])


def _ffn_call(x2, w_gate, w_up, w_down, layer, ln_g, ln_b, alpha, tm):
    T = x2.shape[0]
    rows = pl.BlockSpec((tm, D_MODEL), lambda i: (i, 0))
    vec = pl.BlockSpec((1, D_MODEL), lambda i: (0, 0))
    return pl.pallas_call(
        functools.partial(_ffn_kernel, alpha=alpha),
        grid=(T // tm,),
        in_specs=[
            rows,
            _layer_weight(layer, D_MODEL, D_FF, 1),
            _layer_weight(layer, D_MODEL, D_FF, 1),
            _layer_weight(layer, D_FF, D_MODEL, 1),
            vec, vec,
        ],
        out_specs=rows,
        out_shape=jax.ShapeDtypeStruct((T, D_MODEL), F32),
        compiler_params=_params(("parallel",)),
        name="ffn",
    )(x2, w_gate, w_up, w_down, ln_g, ln_b)


def kernel(x, w_in, gmlp_ln_g, gmlp_ln_b, gmlp_ws, gmlp_bs, hgrn_lb_logits, hgrn_norm_w, fox_fb,
           w_pg, w_pr, w_pa, w_o, ln1_g, ln1_b, w_gate, w_up, w_down, ln2_g, ln2_b):
    batch, seq, _ = x.shape
    depth = w_in.shape[0]
    T = batch * seq
    alpha = float((2 * depth) ** 0.25)
    rows = min(512, seq)
    tq = min(512, seq)
    tm = 512

    o_r = 2 * G_WIDTH
    o_a = o_r + 4 * R_WIDTH
    o_f = o_a + 3 * A_WIDTH
    o_gate = o_f + A_HEADS
    w_uv = w_in[:, :, :o_r].astype(BF16)
    w_r = w_in[:, :, o_r:o_a].astype(BF16)
    w_k = w_in[:, :, o_a + A_WIDTH:o_a + 2 * A_WIDTH].astype(BF16)
    w_qv_t = jnp.concatenate([w_in[:, :, o_a:o_a + A_WIDTH], w_in[:, :, o_a + 2 * A_WIDTH:o_f]],
                             axis=2).astype(BF16).transpose(0, 2, 1)
    spare = LANES - FOX_F_TERMS * A_HEADS
    w_f = jnp.pad(jnp.tile(w_in[:, :, o_f:o_gate], (1, 1, FOX_F_TERMS)), ((0, 0), (0, 0), (0, spare))).astype(BF16)
    w_bgate = w_in[:, :, o_gate:].astype(BF16)
    w_pg_b, w_pr_b, w_pa_b, w_o_b = (w.astype(BF16) for w in (w_pg, w_pr, w_pa, w_o))
    w_gate_b, w_up_b, w_down_b = (w.astype(BF16) for w in (w_gate, w_up, w_down))

    probs = jax.nn.softmax(hgrn_lb_logits.astype(F32), axis=0)
    lbs = jnp.cumsum(probs, axis=0) - probs[0:1]
    fb_pad = jnp.pad(jnp.tile(fox_fb.astype(F32), (1, FOX_F_TERMS)), ((0, 0), (0, spare)))

    x2 = x.reshape(T, D_MODEL)
    for l in range(depth):
        bs_b = jnp.broadcast_to(gmlp_bs[l][:, :, None], (G_GROUPS, G_CHUNK, LANES))
        y_g = _gmlp_call(x2, w_uv, l, gmlp_ln_g[l][None], gmlp_ln_b[l][None], gmlp_ws[l], bs_b, 2 * tm)
        y_r = _hgrn_call(x2, w_r, l, lbs[l][None], hgrn_norm_w[l][None], batch, seq, rows)
        q0, q1, k0, k1, v0, v1 = _fox_proj_call(x2, w_k, w_qv_t, w_f, l, fb_pad[l][None], batch, seq, rows)
        y_a = _fox_attn_call(q0, q1, k0, k1, v0, v1, batch, seq, tq)
        x2 = _merge_call(x2, y_g, y_r, y_a, w_bgate, w_pg_b, w_pr_b, w_pa_b, w_o_b, l,
                         ln1_g[l][None], ln1_b[l][None], alpha, tm)
        x2 = _ffn_call(x2, w_gate_b, w_up_b, w_down_b, l, ln2_g[l][None], ln2_b[l][None], alpha, tm)
    return x2.reshape(batch, seq, D_MODEL)
```

```python
import functools

import numpy as np
import jax
import jax.numpy as jnp
from jax import lax
from jax.experimental import pallas as pl
from jax.experimental.pallas import tpu as pltpu

BF16 = jnp.bfloat16
F32 = jnp.float32

D_MODEL = 1024
G_GROUPS = 4
G_WIDTH = 512
G_CHUNK = 128
R_HEADS = 4
R_DK = 128
R_WIDTH = 512
A_HEADS = 8
A_HEAD_DIM = 64
A_WIDTH = 512
N_BRANCH = 3
D_FF = 2816
LN_EPS = 1e-5
RMS_EPS = 1e-6
NEG_BIG = -1e30
LB_FLOOR = 1e-30

LOG2E = float(np.log2(np.e))
LANES = 128
SUBLANES = 8
HGRN_CHUNK = 128
HGRN_LEVELS = 7
VMEM_LIMIT_BYTES = 56 * 1024 * 1024


def _dot(a, b):
    return jnp.dot(a, b, preferred_element_type=F32)


def _dot_nt(a, b):
    return lax.dot_general(a, b, (((1,), (1,)), ((), ())), preferred_element_type=F32)


def _sigmoid(z):
    return 1.0 / (1.0 + jnp.exp(-z))


def _log_sigmoid(z):
    return jnp.minimum(z, 0.0) - jnp.log(1.0 + jnp.exp(-jnp.abs(z)))


def _gelu(z):
    return 0.5 * z * (1.0 + lax.erf(z * np.float32(2.0 ** -0.5)))


def _layer_norm(r, g, b):
    mu = jnp.mean(r, axis=-1, keepdims=True)
    d = r - mu
    var = jnp.mean(d * d, axis=-1, keepdims=True)
    return d * lax.rsqrt(var + LN_EPS) * g + b


def _split_bf16(a, terms):
    parts = []
    rem = a
    for _ in range(terms):
        p = rem.astype(BF16)
        parts.append(p)
        rem = rem - p.astype(F32)
    return parts


def _layer_weight(layer, rows, cols, grid_rank):
    index_map = (lambda i: (layer, 0, 0)) if grid_rank == 1 else (lambda b, r: (layer, 0, 0))
    return pl.BlockSpec((None, rows, cols), index_map, pipeline_mode=pl.Buffered(1))


def _params(sem):
    return pltpu.CompilerParams(dimension_semantics=sem, vmem_limit_bytes=VMEM_LIMIT_BYTES)


def _gmlp_kernel(x_ref, w_ref, lng_ref, lnb_ref, ws_ref, bs_ref, o_ref):
    tm = x_ref.shape[0]
    h = _dot(x_ref[...].astype(BF16), w_ref[...])
    u = _gelu(h[:, :G_WIDTH])
    v = _layer_norm(_gelu(h[:, G_WIDTH:]), lng_ref[...], lnb_ref[...]).astype(BF16)
    row = lax.broadcasted_iota(jnp.int32, (G_CHUNK, G_CHUNK), 0)
    col = lax.broadcasted_iota(jnp.int32, (G_CHUNK, G_CHUNK), 1)
    for g in range(G_GROUPS):
        w_causal = jnp.where(row >= col, ws_ref[g], 0.0).astype(BF16)
        cs = slice(g * LANES, (g + 1) * LANES)
        for n in range(tm // G_CHUNK):
            rs = slice(n * G_CHUNK, (n + 1) * G_CHUNK)
            sv = _dot(w_causal, v[rs, cs]) + bs_ref[g]
            o_ref[rs, cs] = (u[rs, cs] * sv).astype(BF16)


def _gmlp_call(x2, w_uv, layer, ln_g, ln_b, ws, bs_b, tm):
    T = x2.shape[0]
    return pl.pallas_call(
        _gmlp_kernel,
        grid=(T // tm,),
        in_specs=[
            pl.BlockSpec((tm, D_MODEL), lambda i: (i, 0)),
            _layer_weight(layer, D_MODEL, 2 * G_WIDTH, 1),
            pl.BlockSpec((1, G_WIDTH), lambda i: (0, 0)),
            pl.BlockSpec((1, G_WIDTH), lambda i: (0, 0)),
            pl.BlockSpec((G_GROUPS, G_CHUNK, G_CHUNK), lambda i: (0, 0, 0)),
            pl.BlockSpec((G_GROUPS, G_CHUNK, LANES), lambda i: (0, 0, 0)),
        ],
        out_specs=pl.BlockSpec((tm, G_WIDTH), lambda i: (i, 0)),
        out_shape=jax.ShapeDtypeStruct((T, G_WIDTH), BF16),
        compiler_params=_params(("parallel",)),
        name="gmlp",
    )(x2, w_uv, ln_g, ln_b, ws, bs_b)


def _hgrn_constants():
    c = HGRN_CHUNK
    t = np.arange(c)[:, None]
    tau = np.arange(c)[None, :]
    prefix = (tau <= t).astype(np.float32)
    xor = np.maximum(t ^ tau, 1)
    level_map = np.where(tau < t, np.floor(np.log2(xor)).astype(np.int32),
                         np.where(tau == t, -1, -2)).astype(np.int32)
    return prefix, level_map


def _hgrn_kernel(x_ref, w_ref, lb_ref, nw_ref, tri_ref, lev_ref, o_ref, proj_ref, state_ref):
    rows = x_ref.shape[0]
    c = HGRN_CHUNK

    @pl.when(pl.program_id(1) == 0)
    def _():
        state_ref[...] = jnp.zeros_like(state_ref)

    proj_ref[...] = _dot(x_ref[...].astype(BF16), w_ref[...])
    lb = lb_ref[...]
    log_lb = jnp.log(jnp.maximum(lb, LB_FLOOR))
    log_1m_lb = jnp.log1p(-lb)
    one_m_lb = 1.0 - lb
    norm_w = nw_ref[...]

    def odd_blocks(a, width):
        return jnp.concatenate([a[(2 * g + 1) * width:(2 * g + 2) * width] for g in range(c // (2 * width))], axis=0)

    def gates(ci):
        rs = slice(ci * c, (ci + 1) * c)
        z = proj_ref[rs, R_WIDTH:2 * R_WIDTH]
        b = log_1m_lb + _log_sigmoid(z)
        lf = jnp.maximum(log_lb, b) + jnp.log(1.0 + jnp.exp(-jnp.abs(log_lb - b)))
        key = one_m_lb * _sigmoid(-z)
        lf = lf * np.float32(LOG2E)
        lf_hi, lf_lo = _split_bf16(lf, 2)
        tri = tri_ref[...]
        cum = _dot(tri, lf_hi) + _dot(tri, lf_lo)
        return lf, key, cum

    def chunk(ci, lf, key, cum_all):
        rs = slice(ci * c, (ci + 1) * c)
        lev = lev_ref[...]
        for h in range(R_HEADS):
            hs = slice(h * LANES, (h + 1) * LANES)
            q = proj_ref[rs, h * LANES:(h + 1) * LANES]
            kh = key[:, hs]
            val = proj_ref[rs, 2 * R_WIDTH + h * LANES:2 * R_WIDTH + (h + 1) * LANES]
            gate = proj_ref[rs, 3 * R_WIDTH + h * LANES:3 * R_WIDTH + (h + 1) * LANES]
            q_b = q.astype(BF16)
            k_b = kh.astype(BF16)
            cum = cum_all[:, hs]

            def level_decay(level, cum=cum, hs=hs):
                if level == 0:
                    return lf[:, hs]
                width = 1 << level
                pieces = []
                for g in range(c // (2 * width)):
                    base = 2 * g * width
                    boundary = cum[base + width - 1:base + width]
                    pieces.append(boundary - cum[base:base + width])
                    pieces.append(cum[base + width:base + 2 * width] - boundary)
                return jnp.concatenate(pieces, axis=0)

            scores = jnp.zeros((c, c), F32)
            for level in range(HGRN_LEVELS):
                width = 1 << level
                decay = jnp.exp2(level_decay(level))
                decay_b = decay.astype(BF16)
                k_l = k_b * decay_b if level else k_b
                if width < SUBLANES:
                    s_l = _dot_nt(q_b * decay_b, k_l)
                    scores = jnp.where(lev == level, s_l, scores)
                else:
                    q_l = (odd_blocks(q, width) * odd_blocks(decay, width)).astype(BF16)
                    s_l = _dot_nt(q_l, k_l)
                    rows_out = []
                    for g in range(c // (2 * width)):
                        base = 2 * g * width
                        odd = slice(base + width, base + 2 * width)
                        rows_out.append(scores[base:base + width])
                        rows_out.append(jnp.where(lev[odd] == level, s_l[g * width:(g + 1) * width], scores[odd]))
                    scores = jnp.concatenate(rows_out, axis=0)
            diag = jnp.sum(q * kh, axis=-1, keepdims=True)
            scores = jnp.where(lev == -1, diag, scores)
            val_b = val.astype(BF16)
            intra = _dot(scores.astype(BF16), val_b)
            e_in = jnp.exp2(cum)
            e_out = jnp.exp2(cum[c - 1:c] - cum)
            state = state_ref[h]
            inter = _dot_nt((q * e_in).astype(BF16), state.astype(BF16))
            e_last = e_in[c - 1:c, :]
            state_ref[h] = state * e_last + _dot(val.T.astype(BF16), (kh * e_out).astype(BF16))
            o = inter + intra
            o = o * lax.rsqrt(jnp.mean(o * o, axis=-1, keepdims=True) + RMS_EPS)
            o = o * norm_w * (gate * _sigmoid(gate))
            o_ref[rs, h * LANES:(h + 1) * LANES] = o.astype(BF16)

    for ci in range(rows // c):
        chunk(ci, *gates(ci))


def _hgrn_call(x2, w_r, layer, lb, norm_w, batch, seq, rows):
    T = x2.shape[0]
    nr = seq // rows
    prefix, level_map = _hgrn_constants()
    return pl.pallas_call(
        _hgrn_kernel,
        grid=(batch, nr),
        in_specs=[
            pl.BlockSpec((rows, D_MODEL), lambda b, r: (b * nr + r, 0)),
            _layer_weight(layer, D_MODEL, 4 * R_WIDTH, 2),
            pl.BlockSpec((1, R_WIDTH), lambda b, r: (0, 0)),
            pl.BlockSpec((1, R_DK), lambda b, r: (0, 0)),
            pl.BlockSpec((HGRN_CHUNK, HGRN_CHUNK), lambda b, r: (0, 0)),
            pl.BlockSpec((HGRN_CHUNK, HGRN_CHUNK), lambda b, r: (0, 0)),
        ],
        out_specs=pl.BlockSpec((rows, R_WIDTH), lambda b, r: (b * nr + r, 0)),
        out_shape=jax.ShapeDtypeStruct((T, R_WIDTH), BF16),
        scratch_shapes=[
            pltpu.VMEM((rows, 4 * R_WIDTH), F32),
            pltpu.VMEM((R_HEADS, R_DK, R_DK), F32),
        ],
        compiler_params=_params(("arbitrary", "arbitrary")),
        name="hgrn",
    )(x2, w_r, lb, norm_w, jnp.asarray(prefix, BF16), jnp.asarray(level_map))


FOX_F_TERMS = 3
FOX_PAIRS_PER_STEP = 2
UNIT_Q = 512


def _fox_placement():
    place = np.zeros((LANES, 2 * A_WIDTH), np.float32)
    for h in range(A_HEADS):
        pair, odd = divmod(h, 2)
        spare0 = pair * LANES + (0 if odd else A_HEAD_DIM)
        for t in range(FOX_F_TERMS):
            place[t * A_HEADS + h, odd * A_WIDTH + spare0 + t] = 1.0
    return place


def _fox_pack_pieces(a, lane):
    packed = jnp.zeros(a.shape, F32)
    for t, piece in reversed(list(enumerate(_split_bf16(a, FOX_F_TERMS)))):
        packed = jnp.where(lane < (t + 1) * A_HEADS, piece.astype(F32), packed)
    return packed.astype(BF16)


def _fox_proj_kernel(x_ref, wk_ref, wqv_ref, wf_ref, fb_ref, tri_ref, place_ref,
                     q0_ref, q1_ref, k0_ref, k1_ref, v0_ref, v1_ref, carry_ref):
    rows = x_ref.shape[0]

    @pl.when(pl.program_id(1) == 0)
    def _():
        carry_ref[...] = jnp.zeros_like(carry_ref)

    xb = x_ref[...].astype(BF16)

    lane = lax.broadcasted_iota(jnp.int32, (rows, LANES), 1)
    log_f = _log_sigmoid(_dot(xb, wf_ref[...]) + fb_ref[...])
    k = _dot(xb, wk_ref[...])
    part = _dot(tri_ref[...], _fox_pack_pieces(log_f, lane))
    q_t = _dot_nt(wqv_ref[:A_WIDTH, :], xb) * np.float32(A_HEAD_DIM ** -0.5 * LOG2E)
    cum = part
    for t in range(1, FOX_F_TERMS):
        cum = cum + pltpu.roll(part, t * A_HEADS, axis=1) + pltpu.roll(part, LANES - t * A_HEADS, axis=1)
    cum = cum + carry_ref[...]
    carry_ref[...] = cum[rows - 1:rows, :]
    placed = _dot(_fox_pack_pieces(cum * np.float32(-LOG2E), lane), place_ref[...])
    v_t = _dot_nt(wqv_ref[A_WIDTH:, :], xb)

    even_lane = (lax.broadcasted_iota(jnp.int32, (rows, A_WIDTH), 1) & A_HEAD_DIM) == 0
    k0_ref[...] = jnp.where(even_lane, k, placed[:, :A_WIDTH]).astype(BF16)
    k1_ref[...] = jnp.where(even_lane, placed[:, A_WIDTH:], k).astype(BF16)

    feature = lax.broadcasted_iota(jnp.int32, (A_WIDTH, rows), 0)
    even_row = (feature & A_HEAD_DIM) == 0
    slot = feature & (A_HEAD_DIM - 1)
    f_ones = jnp.where(slot < FOX_F_TERMS, 1.0, 0.0)
    q0_ref[...] = jnp.where(even_row, q_t, f_ones).astype(BF16)
    q1_ref[...] = jnp.where(even_row, f_ones, q_t).astype(BF16)
    v0_ref[...] = jnp.where(even_row, v_t, 1.0).astype(BF16)
    v1_ref[...] = jnp.where(even_row, 1.0, v_t).astype(BF16)


def _fox_proj_call(x2, w_k, w_qv_t, w_f, layer, fb, batch, seq, rows):
    T = x2.shape[0]
    nr = seq // rows
    tri = jnp.asarray(np.tril(np.ones((rows, rows), np.float32)), BF16)
    place = jnp.asarray(_fox_placement(), BF16)
    row_major = jax.ShapeDtypeStruct((T, A_WIDTH), BF16)
    transposed = jax.ShapeDtypeStruct((batch, A_WIDTH, seq), BF16)
    row_block = lambda width: pl.BlockSpec((rows, width), lambda b, r: (b * nr + r, 0))
    col_block = pl.BlockSpec((None, A_WIDTH, rows), lambda b, r: (b, 0, r))
    return pl.pallas_call(
        _fox_proj_kernel,
        grid=(batch, nr),
        in_specs=[
            row_block(D_MODEL),
            _layer_weight(layer, D_MODEL, A_WIDTH, 2),
            _layer_weight(layer, 2 * A_WIDTH, D_MODEL, 2),
            _layer_weight(layer, D_MODEL, LANES, 2),
            pl.BlockSpec((1, LANES), lambda b, r: (0, 0)),
            pl.BlockSpec((rows, rows), lambda b, r: (0, 0)),
            pl.BlockSpec(place.shape, lambda b, r: (0, 0)),
        ],
        out_specs=[col_block, col_block, row_block(A_WIDTH), row_block(A_WIDTH), col_block, col_block],
        out_shape=[transposed, transposed, row_major, row_major, transposed, transposed],
        scratch_shapes=[pltpu.VMEM((1, LANES), F32)],
        compiler_params=_params(("arbitrary", "arbitrary")),
        name="fox_proj",
    )(x2, w_k, w_qv_t, w_f, fb, tri, place)


def _fox_attn_kernel(q0_ref, q1_ref, k0_ref, k1_ref, v0_ref, v1_ref, o_ref,
                     x_ref, xmax_ref, m_ref, acc_ref, *, tile):
    seq = o_ref.shape[0]
    nq = seq // tile
    total = nq * (nq + 1) // 2
    key_pos = lax.broadcasted_iota(jnp.int32, (tile, UNIT_Q), 0)
    qry_pos = lax.broadcasted_iota(jnp.int32, (tile, UNIT_Q), 1)
    q_refs = (q0_ref, q1_ref)
    k_refs = (k0_ref, k1_ref)
    v_refs = (v0_ref, v1_ref)

    n_sub = tile // UNIT_Q
    n_heads = 2 * FOX_PAIRS_PER_STEP
    units = [(h, u) for h in range(n_heads) for u in range(n_sub)]

    def pair_block(h):
        return slice((h // 2) * LANES, (h // 2 + 1) * LANES)

    def scores(i, j, slot, masked, unit):
        h, u = unit
        qc = pl.multiple_of(i * tile + u * UNIT_Q, UNIT_Q)
        kc = pl.multiple_of(j * tile, tile)
        x = _dot(k_refs[h % 2][pl.ds(kc, tile), pair_block(h)], q_refs[h % 2][pair_block(h), pl.ds(qc, UNIT_Q)])
        if masked:
            x = jnp.where(key_pos <= qry_pos + u * UNIT_Q, x, NEG_BIG)
        x_ref[slot, h, u] = x
        xmax_ref[slot, h, u] = jnp.max(x, axis=0, keepdims=True)

    def accumulate(j, slot, unit):
        h, u = unit
        kc = pl.multiple_of(j * tile, tile)
        m = m_ref[h, u]
        m_new = jnp.maximum(m, xmax_ref[slot, h, u])
        p = jnp.exp2((x_ref[slot, h, u] - m_new).astype(BF16))
        acc_ref[h, u] = (jnp.exp2(m - m_new) * acc_ref[h, u]
                         + _dot(v_refs[h % 2][pair_block(h), pl.ds(kc, tile)], p))
        m_ref[h, u] = m_new

    def reset():
        m_ref[...] = jnp.full(m_ref.shape, NEG_BIG, F32)
        acc_ref[...] = jnp.zeros(acc_ref.shape, F32)

    def finalize(i):
        qc = pl.multiple_of(i * tile, tile)
        for pair in range(FOX_PAIRS_PER_STEP):
            a0 = jnp.concatenate([acc_ref[2 * pair, u] for u in range(n_sub)], axis=1)
            a1 = jnp.concatenate([acc_ref[2 * pair + 1, u] for u in range(n_sub)], axis=1)
            out_t = jnp.concatenate([a0[:A_HEAD_DIM] / a0[A_HEAD_DIM:A_HEAD_DIM + 1],
                                     a1[A_HEAD_DIM:] / a1[0:1]], axis=0)
            o_ref[pl.ds(qc, tile), pair * LANES:(pair + 1) * LANES] = out_t.T.astype(BF16)
        reset()

    def advance(i, j):
        wrap = j >= i
        return jnp.where(wrap, i + 1, i), jnp.where(wrap, 0, j + 1)

    def half(t, i, j, slot):
        ni, nj = advance(i, j)
        ci = jnp.minimum(ni, nq - 1)
        cj = jnp.minimum(nj, ci)
        valid = t < total

        for next_diagonal in (False, True):
            @pl.when(valid & ((cj == ci) == next_diagonal))
            def _(next_diagonal=next_diagonal):
                for unit in units:
                    scores(ci, cj, 1 - slot, next_diagonal, unit)
                    accumulate(j, slot, unit)

        @pl.when(valid & (j == i))
        def _():
            finalize(i)

        return ni, nj

    def body(u, carry):
        i, j = carry
        i, j = half(2 * u, i, j, 0)
        return half(2 * u + 1, i, j, 1)

    reset()
    for unit in units:
        scores(0, 0, 0, True, unit)
    lax.fori_loop(0, (total + 1) // 2, body, (jnp.int32(0), jnp.int32(0)))


def _fox_attn_call(q0, q1, k0, k1, v0, v1, batch, seq, tile):
    T = batch * seq
    steps = A_HEADS // (2 * FOX_PAIRS_PER_STEP)
    width = FOX_PAIRS_PER_STEP * LANES
    n_heads = 2 * FOX_PAIRS_PER_STEP
    row_major = pl.BlockSpec((seq, width), lambda b, p: (b, p))
    transposed = pl.BlockSpec((None, width, seq), lambda b, p: (b, p, 0))
    n_sub = tile // UNIT_Q
    return pl.pallas_call(
        functools.partial(_fox_attn_kernel, tile=tile),
        grid=(batch, steps),
        in_specs=[transposed, transposed, row_major, row_major, transposed, transposed],
        out_specs=row_major,
        out_shape=jax.ShapeDtypeStruct((T, A_WIDTH), BF16),
        scratch_shapes=[
            pltpu.VMEM((2, n_heads, n_sub, tile, UNIT_Q), F32),
            pltpu.VMEM((2, n_heads, n_sub, 1, UNIT_Q), F32),
            pltpu.VMEM((n_heads, n_sub, 1, UNIT_Q), F32),
            pltpu.VMEM((n_heads, n_sub, LANES, UNIT_Q), F32),
        ],
        compiler_params=_params(("parallel", "parallel")),
        name="fox_attn",
    )(q0, q1, k0, k1, v0, v1)


def _merge_kernel(x_ref, yg_ref, yr_ref, ya_ref, wgate_ref, wpg_ref, wpr_ref, wpa_ref, wo_ref,
                  g_ref, b_ref, o_ref, *, alpha):
    half = x_ref.shape[0] // 2
    for s in range(2):
        rs = slice(s * half, (s + 1) * half)
        x = x_ref[rs, :]
        xb = x.astype(BF16)
        mixed = None
        for n, (y_ref, wp_ref) in enumerate(((yg_ref, wpg_ref), (yr_ref, wpr_ref), (ya_ref, wpa_ref))):
            gate = _sigmoid(_dot(xb, wgate_ref[:, n * D_MODEL:(n + 1) * D_MODEL]))
            term = gate * _dot(y_ref[rs, :], wp_ref[...])
            mixed = term if mixed is None else mixed + term
        out = _dot(mixed.astype(BF16), wo_ref[...])
        o_ref[rs, :] = _layer_norm(alpha * x + out, g_ref[...], b_ref[...])


def _merge_call(x2, yg, yr, ya, w_gate, w_pg, w_pr, w_pa, w_o, layer, ln_g, ln_b, alpha, tm):
    T = x2.shape[0]
    rows = lambda width: pl.BlockSpec((tm, width), lambda i: (i, 0))
    weight = lambda r, c: _layer_weight(layer, r, c, 1)
    vec = pl.BlockSpec((1, D_MODEL), lambda i: (0, 0))
    return pl.pallas_call(
        functools.partial(_merge_kernel, alpha=alpha),
        grid=(T // tm,),
        in_specs=[
            rows(D_MODEL), rows(G_WIDTH), rows(R_WIDTH), rows(A_WIDTH),
            weight(D_MODEL, N_BRANCH * D_MODEL),
            weight(G_WIDTH, D_MODEL), weight(R_WIDTH, D_MODEL), weight(A_WIDTH, D_MODEL),
            weight(D_MODEL, D_MODEL), vec, vec,
        ],
        out_specs=rows(D_MODEL),
        out_shape=jax.ShapeDtypeStruct((T, D_MODEL), F32),
        compiler_params=_params(("parallel",)),
        name="merge",
    )(x2, yg, yr, ya, w_gate, w_pg, w_pr, w_pa, w_o, ln_g, ln_b)


def _ffn_kernel(x_ref, wg_ref, wu_ref, wd_ref, g_ref, b_ref, o_ref, *, alpha):
    half = x_ref.shape[0] // 2
    halves = [slice(s * half, (s + 1) * half) for s in range(2)]
    hidden = []
    for rs in halves:
        xb = x_ref[rs, :].astype(BF16)
        hg = _dot(xb, wg_ref[...])
        hu = _dot(xb, wu_ref[...])
        hidden.append((hg * _sigmoid(hg) * hu).astype(BF16))
    for rs, hid in zip(halves, hidden):
        out = _dot(hid, wd_ref[...])
        o_ref[rs, :] = _layer_norm(alpha * x_ref[rs, :] + out, g_ref[...], b_ref[...])


def _ffn_call(x2, w_gate, w_up, w_down, layer, ln_g, ln_b, alpha, tm):
    T = x2.shape[0]
    rows = pl.BlockSpec((tm, D_MODEL), lambda i: (i, 0))
    vec = pl.BlockSpec((1, D_MODEL), lambda i: (0, 0))
    return pl.pallas_call(
        functools.partial(_ffn_kernel, alpha=alpha),
        grid=(T // tm,),
        in_specs=[
            rows,
            _layer_weight(layer, D_MODEL, D_FF, 1),
            _layer_weight(layer, D_MODEL, D_FF, 1),
            _layer_weight(layer, D_FF, D_MODEL, 1),
            vec, vec,
        ],
        out_specs=rows,
        out_shape=jax.ShapeDtypeStruct((T, D_MODEL), F32),
        compiler_params=_params(("parallel",)),
        name="ffn",
    )(x2, w_gate, w_up, w_down, ln_g, ln_b)


def kernel(x, w_in, gmlp_ln_g, gmlp_ln_b, gmlp_ws, gmlp_bs, hgrn_lb_logits, hgrn_norm_w, fox_fb,
           w_pg, w_pr, w_pa, w_o, ln1_g, ln1_b, w_gate, w_up, w_down, ln2_g, ln2_b):
    batch, seq, _ = x.shape
    depth = w_in.shape[0]
    T = batch * seq
    alpha = float((2 * depth) ** 0.25)
    rows = min(512, seq)
    tq = min(512, seq)
    tm = 512

    o_r = 2 * G_WIDTH
    o_a = o_r + 4 * R_WIDTH
    o_f = o_a + 3 * A_WIDTH
    o_gate = o_f + A_HEADS
    w_uv = w_in[:, :, :o_r].astype(BF16)
    w_r = w_in[:, :, o_r:o_a].astype(BF16)
    w_k = w_in[:, :, o_a + A_WIDTH:o_a + 2 * A_WIDTH].astype(BF16)
    w_qv_t = jnp.concatenate([w_in[:, :, o_a:o_a + A_WIDTH], w_in[:, :, o_a + 2 * A_WIDTH:o_f]],
                             axis=2).astype(BF16).transpose(0, 2, 1)
    spare = LANES - FOX_F_TERMS * A_HEADS
    w_f = jnp.pad(jnp.tile(w_in[:, :, o_f:o_gate], (1, 1, FOX_F_TERMS)), ((0, 0), (0, 0), (0, spare))).astype(BF16)
    w_bgate = w_in[:, :, o_gate:].astype(BF16)
    w_pg_b, w_pr_b, w_pa_b, w_o_b = (w.astype(BF16) for w in (w_pg, w_pr, w_pa, w_o))
    w_gate_b, w_up_b, w_down_b = (w.astype(BF16) for w in (w_gate, w_up, w_down))

    probs = jax.nn.softmax(hgrn_lb_logits.astype(F32), axis=0)
    lbs = jnp.cumsum(probs, axis=0) - probs[0:1]
    fb_pad = jnp.pad(jnp.tile(fox_fb.astype(F32), (1, FOX_F_TERMS)), ((0, 0), (0, spare)))

    x2 = x.reshape(T, D_MODEL)
    for l in range(depth):
        bs_b = jnp.broadcast_to(gmlp_bs[l][:, :, None], (G_GROUPS, G_CHUNK, LANES))
        y_g = _gmlp_call(x2, w_uv, l, gmlp_ln_g[l][None], gmlp_ln_b[l][None], gmlp_ws[l], bs_b, 2 * tm)
        y_r = _hgrn_call(x2, w_r, l, lbs[l][None], hgrn_norm_w[l][None], batch, seq, rows)
        q0, q1, k0, k1, v0, v1 = _fox_proj_call(x2, w_k, w_qv_t, w_f, l, fb_pad[l][None], batch, seq, rows)
        y_a = _fox_attn_call(q0, q1, k0, k1, v0, v1, batch, seq, tq)
        x2 = _merge_call(x2, y_g, y_r, y_a, w_bgate, w_pg_b, w_pr_b, w_pa_b, w_o_b, l,
                         ln1_g[l][None], ln1_b[l][None], alpha, tm)
        x2 = _ffn_call(x2, w_gate_b, w_up_b, w_down_b, l, ln2_g[l][None], ln2_b[l][None], alpha, tm)
    return x2.reshape(batch, seq, D_MODEL)
```

```python
import functools

import numpy as np
import jax
import jax.numpy as jnp
from jax import lax
from jax.experimental import pallas as pl
from jax.experimental.pallas import tpu as pltpu

BF16 = jnp.bfloat16
F32 = jnp.float32

D_MODEL = 1024
G_GROUPS = 4
G_WIDTH = 512
G_CHUNK = 128
R_HEADS = 4
R_DK = 128
R_WIDTH = 512
A_HEADS = 8
A_HEAD_DIM = 64
A_WIDTH = 512
N_BRANCH = 3
D_FF = 2816
LN_EPS = 1e-5
RMS_EPS = 1e-6
NEG_BIG = -1e30
LB_FLOOR = 1e-30

LOG2E = float(np.log2(np.e))
LANES = 128
SUBLANES = 8
HGRN_CHUNK = 128
HGRN_LEVELS = 7
VMEM_LIMIT_BYTES = 56 * 1024 * 1024


def _dot(a, b):
    return jnp.dot(a, b, preferred_element_type=F32)


def _dot_nt(a, b):
    return lax.dot_general(a, b, (((1,), (1,)), ((), ())), preferred_element_type=F32)


def _dot_tn_nt(w, x):
    return lax.dot_general(w, x, (((0,), (1,)), ((), ())), preferred_element_type=F32)


def _sigmoid(z):
    return 1.0 / (1.0 + jnp.exp(-z))


def _log_sigmoid(z):
    return jnp.minimum(z, 0.0) - jnp.log(1.0 + jnp.exp(-jnp.abs(z)))


def _gelu(z):
    return 0.5 * z * (1.0 + lax.erf(z * np.float32(2.0 ** -0.5)))


def _layer_norm(r, g, b):
    mu = jnp.mean(r, axis=-1, keepdims=True)
    d = r - mu
    var = jnp.mean(d * d, axis=-1, keepdims=True)
    return d * lax.rsqrt(var + LN_EPS) * g + b


def _split_bf16(a, terms):
    parts = []
    rem = a
    for _ in range(terms):
        p = rem.astype(BF16)
        parts.append(p)
        rem = rem - p.astype(F32)
    return parts


def _layer_weight(layer, rows, cols, grid_rank):
    index_map = (lambda i: (layer, 0, 0)) if grid_rank == 1 else (lambda b, r: (layer, 0, 0))
    return pl.BlockSpec((None, rows, cols), index_map, pipeline_mode=pl.Buffered(1))


def _params(sem):
    return pltpu.CompilerParams(dimension_semantics=sem, vmem_limit_bytes=VMEM_LIMIT_BYTES)


def _gmlp_kernel(x_ref, w_ref, lng_ref, lnb_ref, ws_ref, bs_ref, o_ref):
    tm = x_ref.shape[0]
    h = _dot(x_ref[...].astype(BF16), w_ref[...])
    u = _gelu(h[:, :G_WIDTH])
    v = _layer_norm(_gelu(h[:, G_WIDTH:]), lng_ref[...], lnb_ref[...]).astype(BF16)
    row = lax.broadcasted_iota(jnp.int32, (G_CHUNK, G_CHUNK), 0)
    col = lax.broadcasted_iota(jnp.int32, (G_CHUNK, G_CHUNK), 1)
    for g in range(G_GROUPS):
        w_causal = jnp.where(row >= col, ws_ref[g], 0.0).astype(BF16)
        cs = slice(g * LANES, (g + 1) * LANES)
        for n in range(tm // G_CHUNK):
            rs = slice(n * G_CHUNK, (n + 1) * G_CHUNK)
            sv = _dot(w_causal, v[rs, cs]) + bs_ref[g]
            o_ref[rs, cs] = (u[rs, cs] * sv).astype(BF16)


def _gmlp_call(x2, w_uv, layer, ln_g, ln_b, ws, bs_b, tm):
    T = x2.shape[0]
    return pl.pallas_call(
        _gmlp_kernel,
        grid=(T // tm,),
        in_specs=[
            pl.BlockSpec((tm, D_MODEL), lambda i: (i, 0)),
            _layer_weight(layer, D_MODEL, 2 * G_WIDTH, 1),
            pl.BlockSpec((1, G_WIDTH), lambda i: (0, 0)),
            pl.BlockSpec((1, G_WIDTH), lambda i: (0, 0)),
            pl.BlockSpec((G_GROUPS, G_CHUNK, G_CHUNK), lambda i: (0, 0, 0)),
            pl.BlockSpec((G_GROUPS, G_CHUNK, LANES), lambda i: (0, 0, 0)),
        ],
        out_specs=pl.BlockSpec((tm, G_WIDTH), lambda i: (i, 0)),
        out_shape=jax.ShapeDtypeStruct((T, G_WIDTH), BF16),
        compiler_params=_params(("parallel",)),
        name="gmlp",
    )(x2, w_uv, ln_g, ln_b, ws, bs_b)


def _hgrn_constants():
    c = HGRN_CHUNK
    t = np.arange(c)[:, None]
    tau = np.arange(c)[None, :]
    prefix = (tau <= t).astype(np.float32)
    xor = np.maximum(t ^ tau, 1)
    level_map = np.where(tau < t, np.floor(np.log2(xor)).astype(np.int32),
                         np.where(tau == t, -1, -2)).astype(np.int32)
    return prefix, level_map


def _hgrn_kernel(x_ref, w_ref, lb_ref, nw_ref, tri_ref, lev_ref, o_ref, proj_ref, state_ref):
    rows = x_ref.shape[0]
    c = HGRN_CHUNK

    @pl.when(pl.program_id(1) == 0)
    def _():
        state_ref[...] = jnp.zeros_like(state_ref)

    proj_ref[...] = _dot(x_ref[...].astype(BF16), w_ref[...])
    lb = lb_ref[...]
    log_lb = jnp.log(jnp.maximum(lb, LB_FLOOR))
    log_1m_lb = jnp.log1p(-lb)
    one_m_lb = 1.0 - lb
    norm_w = nw_ref[...]

    def odd_blocks(a, width):
        return jnp.concatenate([a[(2 * g + 1) * width:(2 * g + 2) * width] for g in range(c // (2 * width))], axis=0)

    def gates(ci):
        rs = slice(ci * c, (ci + 1) * c)
        z = proj_ref[rs, R_WIDTH:2 * R_WIDTH]
        b = log_1m_lb + _log_sigmoid(z)
        lf = jnp.maximum(log_lb, b) + jnp.log(1.0 + jnp.exp(-jnp.abs(log_lb - b)))
        key = one_m_lb * _sigmoid(-z)
        lf = lf * np.float32(LOG2E)
        lf_hi, lf_lo = _split_bf16(lf, 2)
        tri = tri_ref[...]
        cum = _dot(tri, lf_hi) + _dot(tri, lf_lo)
        return lf, key, cum

    def chunk(ci, lf, key, cum_all):
        rs = slice(ci * c, (ci + 1) * c)
        lev = lev_ref[...]
        for h in range(R_HEADS):
            hs = slice(h * LANES, (h + 1) * LANES)
            q = proj_ref[rs, h * LANES:(h + 1) * LANES]
            kh = key[:, hs]
            val = proj_ref[rs, 2 * R_WIDTH + h * LANES:2 * R_WIDTH + (h + 1) * LANES]
            gate = proj_ref[rs, 3 * R_WIDTH + h * LANES:3 * R_WIDTH + (h + 1) * LANES]
            q_b = q.astype(BF16)
            k_b = kh.astype(BF16)
            cum = cum_all[:, hs]

            def level_decay(level, cum=cum, hs=hs):
                if level == 0:
                    return lf[:, hs]
                width = 1 << level
                pieces = []
                for g in range(c // (2 * width)):
                    base = 2 * g * width
                    boundary = cum[base + width - 1:base + width]
                    pieces.append(boundary - cum[base:base + width])
                    pieces.append(cum[base + width:base + 2 * width] - boundary)
                return jnp.concatenate(pieces, axis=0)

            scores = jnp.zeros((c, c), F32)
            for level in range(HGRN_LEVELS):
                width = 1 << level
                decay = jnp.exp2(level_decay(level))
                decay_b = decay.astype(BF16)
                k_l = k_b * decay_b if level else k_b
                if width < SUBLANES:
                    s_l = _dot_nt(q_b * decay_b, k_l)
                    scores = jnp.where(lev == level, s_l, scores)
                else:
                    q_l = (odd_blocks(q, width) * odd_blocks(decay, width)).astype(BF16)
                    s_l = _dot_nt(q_l, k_l)
                    rows_out = []
                    for g in range(c // (2 * width)):
                        base = 2 * g * width
                        odd = slice(base + width, base + 2 * width)
                        rows_out.append(scores[base:base + width])
                        rows_out.append(jnp.where(lev[odd] == level, s_l[g * width:(g + 1) * width], scores[odd]))
                    scores = jnp.concatenate(rows_out, axis=0)
            diag = jnp.sum(q * kh, axis=-1, keepdims=True)
            scores = jnp.where(lev == -1, diag, scores)
            val_b = val.astype(BF16)
            intra = _dot(scores.astype(BF16), val_b)
            e_in = jnp.exp2(cum)
            e_out = jnp.exp2(cum[c - 1:c] - cum)
            state = state_ref[h]
            inter = _dot_nt((q * e_in).astype(BF16), state.astype(BF16))
            e_last = e_in[c - 1:c, :]
            state_ref[h] = state * e_last + _dot(val.T.astype(BF16), (kh * e_out).astype(BF16))
            o = inter + intra
            o = o * lax.rsqrt(jnp.mean(o * o, axis=-1, keepdims=True) + RMS_EPS)
            o = o * norm_w * (gate * _sigmoid(gate))
            o_ref[rs, h * LANES:(h + 1) * LANES] = o.astype(BF16)

    for ci in range(rows // c):
        chunk(ci, *gates(ci))


def _hgrn_call(x2, w_r, layer, lb, norm_w, batch, seq, rows):
    T = x2.shape[0]
    nr = seq // rows
    prefix, level_map = _hgrn_constants()
    return pl.pallas_call(
        _hgrn_kernel,
        grid=(batch, nr),
        in_specs=[
            pl.BlockSpec((rows, D_MODEL), lambda b, r: (b * nr + r, 0)),
            _layer_weight(layer, D_MODEL, 4 * R_WIDTH, 2),
            pl.BlockSpec((1, R_WIDTH), lambda b, r: (0, 0)),
            pl.BlockSpec((1, R_DK), lambda b, r: (0, 0)),
            pl.BlockSpec((HGRN_CHUNK, HGRN_CHUNK), lambda b, r: (0, 0)),
            pl.BlockSpec((HGRN_CHUNK, HGRN_CHUNK), lambda b, r: (0, 0)),
        ],
        out_specs=pl.BlockSpec((rows, R_WIDTH), lambda b, r: (b * nr + r, 0)),
        out_shape=jax.ShapeDtypeStruct((T, R_WIDTH), BF16),
        scratch_shapes=[
            pltpu.VMEM((rows, 4 * R_WIDTH), F32),
            pltpu.VMEM((R_HEADS, R_DK, R_DK), F32),
        ],
        compiler_params=_params(("arbitrary", "arbitrary")),
        name="hgrn",
    )(x2, w_r, lb, norm_w, jnp.asarray(prefix, BF16), jnp.asarray(level_map))


FOX_F_TERMS = 3
FOX_PAIRS_PER_STEP = 2
UNIT_Q = 512


def _fox_placement():
    place = np.zeros((LANES, 2 * A_WIDTH), np.float32)
    for h in range(A_HEADS):
        pair, odd = divmod(h, 2)
        spare0 = pair * LANES + (0 if odd else A_HEAD_DIM)
        for t in range(FOX_F_TERMS):
            place[t * A_HEADS + h, odd * A_WIDTH + spare0 + t] = 1.0
    return place


def _fox_pack_pieces(a, lane):
    packed = jnp.zeros(a.shape, F32)
    for t, piece in reversed(list(enumerate(_split_bf16(a, FOX_F_TERMS)))):
        packed = jnp.where(lane < (t + 1) * A_HEADS, piece.astype(F32), packed)
    return packed.astype(BF16)


def _fox_proj_kernel(x_ref, wk_ref, wqv_ref, wf_ref, fb_ref, tri_ref, place_ref,
                     q0_ref, q1_ref, k0_ref, k1_ref, v0_ref, v1_ref, carry_ref):
    rows = x_ref.shape[0]

    @pl.when(pl.program_id(1) == 0)
    def _():
        carry_ref[...] = jnp.zeros_like(carry_ref)

    xb = x_ref[...].astype(BF16)

    lane = lax.broadcasted_iota(jnp.int32, (rows, LANES), 1)
    log_f = _log_sigmoid(_dot(xb, wf_ref[...]) + fb_ref[...])
    k = _dot(xb, wk_ref[...])
    part = _dot(tri_ref[...], _fox_pack_pieces(log_f, lane))
    q_t = _dot_tn_nt(wqv_ref[:, :A_WIDTH], xb) * np.float32(A_HEAD_DIM ** -0.5 * LOG2E)
    cum = part
    for t in range(1, FOX_F_TERMS):
        cum = cum + pltpu.roll(part, t * A_HEADS, axis=1) + pltpu.roll(part, LANES - t * A_HEADS, axis=1)
    cum = cum + carry_ref[...]
    carry_ref[...] = cum[rows - 1:rows, :]
    placed = _dot(_fox_pack_pieces(cum * np.float32(-LOG2E), lane), place_ref[...])
    v_t = _dot_tn_nt(wqv_ref[:, A_WIDTH:], xb)

    even_lane = (lax.broadcasted_iota(jnp.int32, (rows, A_WIDTH), 1) & A_HEAD_DIM) == 0
    k0_ref[...] = jnp.where(even_lane, k, placed[:, :A_WIDTH]).astype(BF16)
    k1_ref[...] = jnp.where(even_lane, placed[:, A_WIDTH:], k).astype(BF16)

    feature = lax.broadcasted_iota(jnp.int32, (A_WIDTH, rows), 0)
    even_row = (feature & A_HEAD_DIM) == 0
    slot = feature & (A_HEAD_DIM - 1)
    f_ones = jnp.where(slot < FOX_F_TERMS, 1.0, 0.0)
    q0_ref[...] = jnp.where(even_row, q_t, f_ones).astype(BF16)
    q1_ref[...] = jnp.where(even_row, f_ones, q_t).astype(BF16)
    v0_ref[...] = jnp.where(even_row, v_t, 1.0).astype(BF16)
    v1_ref[...] = jnp.where(even_row, 1.0, v_t).astype(BF16)


def _fox_proj_call(x2, w_k, w_qv, w_f, layer, fb, batch, seq, rows):
    T = x2.shape[0]
    nr = seq // rows
    tri = jnp.asarray(np.tril(np.ones((rows, rows), np.float32)), BF16)
    place = jnp.asarray(_fox_placement(), BF16)
    row_major = jax.ShapeDtypeStruct((T, A_WIDTH), BF16)
    transposed = jax.ShapeDtypeStruct((batch, A_WIDTH, seq), BF16)
    row_block = lambda width: pl.BlockSpec((rows, width), lambda b, r: (b * nr + r, 0))
    col_block = pl.BlockSpec((None, A_WIDTH, rows), lambda b, r: (b, 0, r))
    return pl.pallas_call(
        _fox_proj_kernel,
        grid=(batch, nr),
        in_specs=[
            row_block(D_MODEL),
            _layer_weight(layer, D_MODEL, A_WIDTH, 2),
            _layer_weight(layer, D_MODEL, 2 * A_WIDTH, 2),
            _layer_weight(layer, D_MODEL, LANES, 2),
            pl.BlockSpec((1, LANES), lambda b, r: (0, 0)),
            pl.BlockSpec((rows, rows), lambda b, r: (0, 0)),
            pl.BlockSpec(place.shape, lambda b, r: (0, 0)),
        ],
        out_specs=[col_block, col_block, row_block(A_WIDTH), row_block(A_WIDTH), col_block, col_block],
        out_shape=[transposed, transposed, row_major, row_major, transposed, transposed],
        scratch_shapes=[pltpu.VMEM((1, LANES), F32)],
        compiler_params=_params(("arbitrary", "arbitrary")),
        name="fox_proj",
    )(x2, w_k, w_qv, w_f, fb, tri, place)


def _fox_attn_kernel(q0_ref, q1_ref, k0_ref, k1_ref, v0_ref, v1_ref, o_ref,
                     x_ref, xmax_ref, m_ref, acc_ref, *, tile):
    seq = o_ref.shape[0]
    nq = seq // tile
    total = nq * (nq + 1) // 2
    key_pos = lax.broadcasted_iota(jnp.int32, (tile, UNIT_Q), 0)
    qry_pos = lax.broadcasted_iota(jnp.int32, (tile, UNIT_Q), 1)
    q_refs = (q0_ref, q1_ref)
    k_refs = (k0_ref, k1_ref)
    v_refs = (v0_ref, v1_ref)

    n_sub = tile // UNIT_Q
    n_heads = 2 * FOX_PAIRS_PER_STEP
    units = [(h, u) for h in range(n_heads) for u in range(n_sub)]

    def pair_block(h):
        return slice((h // 2) * LANES, (h // 2 + 1) * LANES)

    def scores(i, j, slot, masked, unit):
        h, u = unit
        qc = pl.multiple_of(i * tile + u * UNIT_Q, UNIT_Q)
        kc = pl.multiple_of(j * tile, tile)
        x = _dot(k_refs[h % 2][pl.ds(kc, tile), pair_block(h)], q_refs[h % 2][pair_block(h), pl.ds(qc, UNIT_Q)])
        if masked:
            x = jnp.where(key_pos <= qry_pos + u * UNIT_Q, x, NEG_BIG)
        x_ref[slot, h, u] = x
        xmax_ref[slot, h, u] = jnp.max(x, axis=0, keepdims=True)

    def accumulate(j, slot, unit):
        h, u = unit
        kc = pl.multiple_of(j * tile, tile)
        m = m_ref[h, u]
        m_new = jnp.maximum(m, xmax_ref[slot, h, u])
        p = jnp.exp2((x_ref[slot, h, u] - m_new).astype(BF16))
        acc_ref[h, u] = (jnp.exp2(m - m_new) * acc_ref[h, u]
                         + _dot(v_refs[h % 2][pair_block(h), pl.ds(kc, tile)], p))
        m_ref[h, u] = m_new

    def reset():
        m_ref[...] = jnp.full(m_ref.shape, NEG_BIG, F32)
        acc_ref[...] = jnp.zeros(acc_ref.shape, F32)

    def finalize(i):
        qc = pl.multiple_of(i * tile, tile)
        for pair in range(FOX_PAIRS_PER_STEP):
            a0 = jnp.concatenate([acc_ref[2 * pair, u] for u in range(n_sub)], axis=1)
            a1 = jnp.concatenate([acc_ref[2 * pair + 1, u] for u in range(n_sub)], axis=1)
            out_t = jnp.concatenate([a0[:A_HEAD_DIM] / a0[A_HEAD_DIM:A_HEAD_DIM + 1],
                                     a1[A_HEAD_DIM:] / a1[0:1]], axis=0)
            o_ref[pl.ds(qc, tile), pair * LANES:(pair + 1) * LANES] = out_t.T.astype(BF16)
        reset()

    def advance(i, j):
        wrap = j >= i
        return jnp.where(wrap, i + 1, i), jnp.where(wrap, 0, j + 1)

    def half(t, i, j, slot):
        ni, nj = advance(i, j)
        ci = jnp.minimum(ni, nq - 1)
        cj = jnp.minimum(nj, ci)
        valid = t < total

        for next_diagonal in (False, True):
            @pl.when(valid & ((cj == ci) == next_diagonal))
            def _(next_diagonal=next_diagonal):
                for unit in units:
                    scores(ci, cj, 1 - slot, next_diagonal, unit)
                    accumulate(j, slot, unit)

        @pl.when(valid & (j == i))
        def _():
            finalize(i)

        return ni, nj

    def body(u, carry):
        i, j = carry
        i, j = half(2 * u, i, j, 0)
        return half(2 * u + 1, i, j, 1)

    reset()
    for unit in units:
        scores(0, 0, 0, True, unit)
    lax.fori_loop(0, (total + 1) // 2, body, (jnp.int32(0), jnp.int32(0)))


def _fox_attn_call(q0, q1, k0, k1, v0, v1, batch, seq, tile):
    T = batch * seq
    steps = A_HEADS // (2 * FOX_PAIRS_PER_STEP)
    width = FOX_PAIRS_PER_STEP * LANES
    n_heads = 2 * FOX_PAIRS_PER_STEP
    row_major = pl.BlockSpec((seq, width), lambda b, p: (b, p))
    transposed = pl.BlockSpec((None, width, seq), lambda b, p: (b, p, 0))
    n_sub = tile // UNIT_Q
    return pl.pallas_call(
        functools.partial(_fox_attn_kernel, tile=tile),
        grid=(batch, steps),
        in_specs=[transposed, transposed, row_major, row_major, transposed, transposed],
        out_specs=row_major,
        out_shape=jax.ShapeDtypeStruct((T, A_WIDTH), BF16),
        scratch_shapes=[
            pltpu.VMEM((2, n_heads, n_sub, tile, UNIT_Q), F32),
            pltpu.VMEM((2, n_heads, n_sub, 1, UNIT_Q), F32),
            pltpu.VMEM((n_heads, n_sub, 1, UNIT_Q), F32),
            pltpu.VMEM((n_heads, n_sub, LANES, UNIT_Q), F32),
        ],
        compiler_params=_params(("parallel", "parallel")),
        name="fox_attn",
    )(q0, q1, k0, k1, v0, v1)


def _merge_kernel(x_ref, yg_ref, yr_ref, ya_ref, wgate_ref, wpg_ref, wpr_ref, wpa_ref, wo_ref,
                  g_ref, b_ref, o_ref, *, alpha):
    half = x_ref.shape[0] // 2
    for s in range(2):
        rs = slice(s * half, (s + 1) * half)
        x = x_ref[rs, :]
        xb = x.astype(BF16)
        mixed = None
        for n, (y_ref, wp_ref) in enumerate(((yg_ref, wpg_ref), (yr_ref, wpr_ref), (ya_ref, wpa_ref))):
            gate = _sigmoid(_dot(xb, wgate_ref[:, n * D_MODEL:(n + 1) * D_MODEL]))
            term = gate * _dot(y_ref[rs, :], wp_ref[...])
            mixed = term if mixed is None else mixed + term
        out = _dot(mixed.astype(BF16), wo_ref[...])
        o_ref[rs, :] = _layer_norm(alpha * x + out, g_ref[...], b_ref[...])


def _merge_call(x2, yg, yr, ya, w_gate, w_pg, w_pr, w_pa, w_o, layer, ln_g, ln_b, alpha, tm):
    T = x2.shape[0]
    rows = lambda width: pl.BlockSpec((tm, width), lambda i: (i, 0))
    weight = lambda r, c: _layer_weight(layer, r, c, 1)
    vec = pl.BlockSpec((1, D_MODEL), lambda i: (0, 0))
    return pl.pallas_call(
        functools.partial(_merge_kernel, alpha=alpha),
        grid=(T // tm,),
        in_specs=[
            rows(D_MODEL), rows(G_WIDTH), rows(R_WIDTH), rows(A_WIDTH),
            weight(D_MODEL, N_BRANCH * D_MODEL),
            weight(G_WIDTH, D_MODEL), weight(R_WIDTH, D_MODEL), weight(A_WIDTH, D_MODEL),
            weight(D_MODEL, D_MODEL), vec, vec,
        ],
        out_specs=rows(D_MODEL),
        out_shape=jax.ShapeDtypeStruct((T, D_MODEL), F32),
        compiler_params=_params(("parallel",)),
        name="merge",
    )(x2, yg, yr, ya, w_gate, w_pg, w_pr, w_pa, w_o, ln_g, ln_b)


def _ffn_kernel(x_ref, wg_ref, wu_ref, wd_ref, g_ref, b_ref, o_ref, *, alpha):
    half = x_ref.shape[0] // 2
    halves = [slice(s * half, (s + 1) * half) for s in range(2)]
    hidden = []
    for rs in halves:
        xb = x_ref[rs, :].astype(BF16)
        hg = _dot(xb, wg_ref[...])
        hu = _dot(xb, wu_ref[...])
        hidden.append((hg * _sigmoid(hg) * hu).astype(BF16))
    for rs, hid in zip(halves, hidden):
        out = _dot(hid, wd_ref[...])
        o_ref[rs, :] = _layer_norm(alpha * x_ref[rs, :] + out, g_ref[...], b_ref[...])


def _ffn_call(x2, w_gate, w_up, w_down, layer, ln_g, ln_b, alpha, tm):
    T = x2.shape[0]
    rows = pl.BlockSpec((tm, D_MODEL), lambda i: (i, 0))
    vec = pl.BlockSpec((1, D_MODEL), lambda i: (0, 0))
    return pl.pallas_call(
        functools.partial(_ffn_kernel, alpha=alpha),
        grid=(T // tm,),
        in_specs=[
            rows,
            _layer_weight(layer, D_MODEL, D_FF, 1),
            _layer_weight(layer, D_MODEL, D_FF, 1),
            _layer_weight(layer, D_FF, D_MODEL, 1),
            vec, vec,
        ],
        out_specs=rows,
        out_shape=jax.ShapeDtypeStruct((T, D_MODEL), F32),
        compiler_params=_params(("parallel",)),
        name="ffn",
    )(x2, w_gate, w_up, w_down, ln_g, ln_b)


def kernel(x, w_in, gmlp_ln_g, gmlp_ln_b, gmlp_ws, gmlp_bs, hgrn_lb_logits, hgrn_norm_w, fox_fb,
           w_pg, w_pr, w_pa, w_o, ln1_g, ln1_b, w_gate, w_up, w_down, ln2_g, ln2_b):
    batch, seq, _ = x.shape
    depth = w_in.shape[0]
    T = batch * seq
    alpha = float((2 * depth) ** 0.25)
    rows = min(512, seq)
    tq = min(512, seq)
    tm = 512

    o_r = 2 * G_WIDTH
    o_a = o_r + 4 * R_WIDTH
    o_f = o_a + 3 * A_WIDTH
    o_gate = o_f + A_HEADS
    w_uv = w_in[:, :, :o_r].astype(BF16)
    w_r = w_in[:, :, o_r:o_a].astype(BF16)
    w_k = w_in[:, :, o_a + A_WIDTH:o_a + 2 * A_WIDTH].astype(BF16)
    w_qv = jnp.concatenate([w_in[:, :, o_a:o_a + A_WIDTH], w_in[:, :, o_a + 2 * A_WIDTH:o_f]],
                           axis=2).astype(BF16)
    spare = LANES - FOX_F_TERMS * A_HEADS
    w_f = jnp.pad(jnp.tile(w_in[:, :, o_f:o_gate], (1, 1, FOX_F_TERMS)), ((0, 0), (0, 0), (0, spare))).astype(BF16)
    w_bgate = w_in[:, :, o_gate:].astype(BF16)
    w_pg_b, w_pr_b, w_pa_b, w_o_b = (w.astype(BF16) for w in (w_pg, w_pr, w_pa, w_o))
    w_gate_b, w_up_b, w_down_b = (w.astype(BF16) for w in (w_gate, w_up, w_down))

    probs = jax.nn.softmax(hgrn_lb_logits.astype(F32), axis=0)
    lbs = jnp.cumsum(probs, axis=0) - probs[0:1]
    fb_pad = jnp.pad(jnp.tile(fox_fb.astype(F32), (1, FOX_F_TERMS)), ((0, 0), (0, spare)))

    x2 = x.reshape(T, D_MODEL)
    for l in range(depth):
        bs_b = jnp.broadcast_to(gmlp_bs[l][:, :, None], (G_GROUPS, G_CHUNK, LANES))
        y_g = _gmlp_call(x2, w_uv, l, gmlp_ln_g[l][None], gmlp_ln_b[l][None], gmlp_ws[l], bs_b, 2 * tm)
        y_r = _hgrn_call(x2, w_r, l, lbs[l][None], hgrn_norm_w[l][None], batch, seq, rows)
        q0, q1, k0, k1, v0, v1 = _fox_proj_call(x2, w_k, w_qv, w_f, l, fb_pad[l][None], batch, seq, rows)
        y_a = _fox_attn_call(q0, q1, k0, k1, v0, v1, batch, seq, tq)
        x2 = _merge_call(x2, y_g, y_r, y_a, w_bgate, w_pg_b, w_pr_b, w_pa_b, w_o_b, l,
                         ln1_g[l][None], ln1_b[l][None], alpha, tm)
        x2 = _ffn_call(x2, w_gate_b, w_up_b, w_down_b, l, ln2_g[l][None], ln2_b[l][None], alpha, tm)
    return x2.reshape(batch, seq, D_MODEL)
```

```python
import functools

import numpy as np
import jax
import jax.numpy as jnp
from jax import lax
from jax.experimental import pallas as pl
from jax.experimental.pallas import tpu as pltpu

BF16 = jnp.bfloat16
F32 = jnp.float32

D_MODEL = 1024
G_GROUPS = 4
G_WIDTH = 512
G_CHUNK = 128
R_HEADS = 4
R_DK = 128
R_WIDTH = 512
A_HEADS = 8
A_HEAD_DIM = 64
A_WIDTH = 512
N_BRANCH = 3
D_FF = 2816
LN_EPS = 1e-5
RMS_EPS = 1e-6
NEG_BIG = -1e30
LB_FLOOR = 1e-30

LOG2E = float(np.log2(np.e))
LANES = 128
SUBLANES = 8
HGRN_CHUNK = 128
HGRN_LEVELS = 7
VMEM_LIMIT_BYTES = 56 * 1024 * 1024


def _dot(a, b):
    return jnp.dot(a, b, preferred_element_type=F32)


def _dot_nt(a, b):
    return lax.dot_general(a, b, (((1,), (1,)), ((), ())), preferred_element_type=F32)


def _dot_tn_nt(w, x):
    return lax.dot_general(w, x, (((0,), (1,)), ((), ())), preferred_element_type=F32)


def _sigmoid(z):
    return 1.0 / (1.0 + jnp.exp(-z))


def _log_sigmoid(z):
    return jnp.minimum(z, 0.0) - jnp.log(1.0 + jnp.exp(-jnp.abs(z)))


def _gelu(z):
    return 0.5 * z * (1.0 + lax.erf(z * np.float32(2.0 ** -0.5)))


def _layer_norm(r, g, b):
    mu = jnp.mean(r, axis=-1, keepdims=True)
    d = r - mu
    var = jnp.mean(d * d, axis=-1, keepdims=True)
    return d * lax.rsqrt(var + LN_EPS) * g + b


def _split_bf16(a, terms):
    parts = []
    rem = a
    for _ in range(terms):
        p = rem.astype(BF16)
        parts.append(p)
        rem = rem - p.astype(F32)
    return parts


def _layer_weight(layer, rows, cols, grid_rank):
    index_map = (lambda i: (layer, 0, 0)) if grid_rank == 1 else (lambda b, r: (layer, 0, 0))
    return pl.BlockSpec((None, rows, cols), index_map, pipeline_mode=pl.Buffered(1))


def _params(sem):
    return pltpu.CompilerParams(dimension_semantics=sem, vmem_limit_bytes=VMEM_LIMIT_BYTES)


def _gmlp_kernel(x_ref, w_ref, lng_ref, lnb_ref, ws_ref, bs_ref, o_ref):
    tm = x_ref.shape[0]
    h = _dot(x_ref[...].astype(BF16), w_ref[...])
    u = _gelu(h[:, :G_WIDTH])
    v = _layer_norm(_gelu(h[:, G_WIDTH:]), lng_ref[...], lnb_ref[...]).astype(BF16)
    row = lax.broadcasted_iota(jnp.int32, (G_CHUNK, G_CHUNK), 0)
    col = lax.broadcasted_iota(jnp.int32, (G_CHUNK, G_CHUNK), 1)
    for g in range(G_GROUPS):
        w_causal = jnp.where(row >= col, ws_ref[g], 0.0).astype(BF16)
        cs = slice(g * LANES, (g + 1) * LANES)
        for n in range(tm // G_CHUNK):
            rs = slice(n * G_CHUNK, (n + 1) * G_CHUNK)
            sv = _dot(w_causal, v[rs, cs]) + bs_ref[g]
            o_ref[rs, cs] = (u[rs, cs] * sv).astype(BF16)


def _gmlp_call(x2, w_uv, layer, ln_g, ln_b, ws, bs_b, tm):
    T = x2.shape[0]
    return pl.pallas_call(
        _gmlp_kernel,
        grid=(T // tm,),
        in_specs=[
            pl.BlockSpec((tm, D_MODEL), lambda i: (i, 0)),
            _layer_weight(layer, D_MODEL, 2 * G_WIDTH, 1),
            pl.BlockSpec((1, G_WIDTH), lambda i: (0, 0)),
            pl.BlockSpec((1, G_WIDTH), lambda i: (0, 0)),
            pl.BlockSpec((G_GROUPS, G_CHUNK, G_CHUNK), lambda i: (0, 0, 0)),
            pl.BlockSpec((G_GROUPS, G_CHUNK, LANES), lambda i: (0, 0, 0)),
        ],
        out_specs=pl.BlockSpec((tm, G_WIDTH), lambda i: (i, 0)),
        out_shape=jax.ShapeDtypeStruct((T, G_WIDTH), BF16),
        compiler_params=_params(("parallel",)),
        name="gmlp",
    )(x2, w_uv, ln_g, ln_b, ws, bs_b)


def _hgrn_constants():
    c = HGRN_CHUNK
    t = np.arange(c)[:, None]
    tau = np.arange(c)[None, :]
    prefix = (tau <= t).astype(np.float32)
    xor = np.maximum(t ^ tau, 1)
    level_map = np.where(tau < t, np.floor(np.log2(xor)).astype(np.int32),
                         np.where(tau == t, -1, -2)).astype(np.int32)
    return prefix, level_map


def _hgrn_kernel(x_ref, w_ref, lb_ref, nw_ref, tri_ref, lev_ref, o_ref, proj_ref, state_ref):
    rows = x_ref.shape[0]
    c = HGRN_CHUNK

    @pl.when(pl.program_id(1) == 0)
    def _():
        state_ref[...] = jnp.zeros_like(state_ref)

    proj_ref[...] = _dot(x_ref[...].astype(BF16), w_ref[...])
    lb = lb_ref[...]
    log_lb = jnp.log(jnp.maximum(lb, LB_FLOOR))
    log_1m_lb = jnp.log1p(-lb)
    one_m_lb = 1.0 - lb
    norm_w = nw_ref[...]

    def odd_blocks(a, width):
        return jnp.concatenate([a[(2 * g + 1) * width:(2 * g + 2) * width] for g in range(c // (2 * width))], axis=0)

    def gates(ci):
        rs = slice(ci * c, (ci + 1) * c)
        z = proj_ref[rs, R_WIDTH:2 * R_WIDTH]
        b = log_1m_lb + _log_sigmoid(z)
        lf = jnp.maximum(log_lb, b) + jnp.log(1.0 + jnp.exp(-jnp.abs(log_lb - b)))
        key = one_m_lb * _sigmoid(-z)
        lf = lf * np.float32(LOG2E)
        lf_hi, lf_lo = _split_bf16(lf, 2)
        tri = tri_ref[...]
        cum = _dot(tri, lf_hi) + _dot(tri, lf_lo)
        return lf, key, cum

    def chunk(ci, lf, key, cum_all):
        rs = slice(ci * c, (ci + 1) * c)
        lev = lev_ref[...]
        for h in range(R_HEADS):
            hs = slice(h * LANES, (h + 1) * LANES)
            q = proj_ref[rs, h * LANES:(h + 1) * LANES]
            kh = key[:, hs]
            val = proj_ref[rs, 2 * R_WIDTH + h * LANES:2 * R_WIDTH + (h + 1) * LANES]
            gate = proj_ref[rs, 3 * R_WIDTH + h * LANES:3 * R_WIDTH + (h + 1) * LANES]
            q_b = q.astype(BF16)
            k_b = kh.astype(BF16)
            cum = cum_all[:, hs]

            def level_decay(level, cum=cum, hs=hs):
                if level == 0:
                    return lf[:, hs]
                width = 1 << level
                pieces = []
                for g in range(c // (2 * width)):
                    base = 2 * g * width
                    boundary = cum[base + width - 1:base + width]
                    pieces.append(boundary - cum[base:base + width])
                    pieces.append(cum[base + width:base + 2 * width] - boundary)
                return jnp.concatenate(pieces, axis=0)

            scores = jnp.zeros((c, c), F32)
            for level in range(HGRN_LEVELS):
                width = 1 << level
                decay = jnp.exp2(level_decay(level))
                decay_b = decay.astype(BF16)
                k_l = k_b * decay_b if level else k_b
                if width < SUBLANES:
                    s_l = _dot_nt(q_b * decay_b, k_l)
                    scores = jnp.where(lev == level, s_l, scores)
                else:
                    q_l = (odd_blocks(q, width) * odd_blocks(decay, width)).astype(BF16)
                    s_l = _dot_nt(q_l, k_l)
                    rows_out = []
                    for g in range(c // (2 * width)):
                        base = 2 * g * width
                        odd = slice(base + width, base + 2 * width)
                        rows_out.append(scores[base:base + width])
                        rows_out.append(jnp.where(lev[odd] == level, s_l[g * width:(g + 1) * width], scores[odd]))
                    scores = jnp.concatenate(rows_out, axis=0)
            diag = jnp.sum(q * kh, axis=-1, keepdims=True)
            scores = jnp.where(lev == -1, diag, scores)
            val_b = val.astype(BF16)
            intra = _dot(scores.astype(BF16), val_b)
            e_in = jnp.exp2(cum)
            e_out = jnp.exp2(cum[c - 1:c] - cum)
            state = state_ref[h]
            inter = _dot_nt((q * e_in).astype(BF16), state.astype(BF16))
            e_last = e_in[c - 1:c, :]
            state_ref[h] = state * e_last + _dot(val.T.astype(BF16), (kh * e_out).astype(BF16))
            o = inter + intra
            o = o * lax.rsqrt(jnp.mean(o * o, axis=-1, keepdims=True) + RMS_EPS)
            o = o * norm_w * (gate * _sigmoid(gate))
            o_ref[rs, h * LANES:(h + 1) * LANES] = o.astype(BF16)

    for ci in range(rows // c):
        chunk(ci, *gates(ci))


def _hgrn_call(x2, w_r, layer, lb, norm_w, batch, seq, rows):
    T = x2.shape[0]
    nr = seq // rows
    prefix, level_map = _hgrn_constants()
    return pl.pallas_call(
        _hgrn_kernel,
        grid=(batch, nr),
        in_specs=[
            pl.BlockSpec((rows, D_MODEL), lambda b, r: (b * nr + r, 0)),
            _layer_weight(layer, D_MODEL, 4 * R_WIDTH, 2),
            pl.BlockSpec((1, R_WIDTH), lambda b, r: (0, 0)),
            pl.BlockSpec((1, R_DK), lambda b, r: (0, 0)),
            pl.BlockSpec((HGRN_CHUNK, HGRN_CHUNK), lambda b, r: (0, 0)),
            pl.BlockSpec((HGRN_CHUNK, HGRN_CHUNK), lambda b, r: (0, 0)),
        ],
        out_specs=pl.BlockSpec((rows, R_WIDTH), lambda b, r: (b * nr + r, 0)),
        out_shape=jax.ShapeDtypeStruct((T, R_WIDTH), BF16),
        scratch_shapes=[
            pltpu.VMEM((rows, 4 * R_WIDTH), F32),
            pltpu.VMEM((R_HEADS, R_DK, R_DK), F32),
        ],
        compiler_params=_params(("arbitrary", "arbitrary")),
        name="hgrn",
    )(x2, w_r, lb, norm_w, jnp.asarray(prefix, BF16), jnp.asarray(level_map))


FOX_F_TERMS = 3
FOX_PAIRS_PER_STEP = 2
UNIT_Q = 512


def _fox_placement():
    place = np.zeros((LANES, 2 * A_WIDTH), np.float32)
    for h in range(A_HEADS):
        pair, odd = divmod(h, 2)
        spare0 = pair * LANES + (0 if odd else A_HEAD_DIM)
        for t in range(FOX_F_TERMS):
            place[t * A_HEADS + h, odd * A_WIDTH + spare0 + t] = 1.0
    return place


def _fox_pack_pieces(a, lane):
    packed = jnp.zeros(a.shape, F32)
    for t, piece in reversed(list(enumerate(_split_bf16(a, FOX_F_TERMS)))):
        packed = jnp.where(lane < (t + 1) * A_HEADS, piece.astype(F32), packed)
    return packed.astype(BF16)


def _fox_proj_kernel(x_ref, wk_ref, wqv_ref, wf_ref, fb_ref, tri_ref, place_ref,
                     q0_ref, q1_ref, k0_ref, k1_ref, v0_ref, v1_ref, carry_ref):
    rows = x_ref.shape[0]

    @pl.when(pl.program_id(1) == 0)
    def _():
        carry_ref[...] = jnp.zeros_like(carry_ref)

    xb = x_ref[...].astype(BF16)

    lane = lax.broadcasted_iota(jnp.int32, (rows, LANES), 1)
    log_f = _log_sigmoid(_dot(xb, wf_ref[...]) + fb_ref[...])
    k = _dot(xb, wk_ref[...])
    part = _dot(tri_ref[...], _fox_pack_pieces(log_f, lane))
    q_t = _dot_tn_nt(wqv_ref[:, :A_WIDTH], xb) * np.float32(A_HEAD_DIM ** -0.5 * LOG2E)
    cum = part
    for t in range(1, FOX_F_TERMS):
        cum = cum + pltpu.roll(part, t * A_HEADS, axis=1) + pltpu.roll(part, LANES - t * A_HEADS, axis=1)
    cum = cum + carry_ref[...]
    carry_ref[...] = cum[rows - 1:rows, :]
    placed = _dot(_fox_pack_pieces(cum * np.float32(-LOG2E), lane), place_ref[...])
    v_t = _dot_tn_nt(wqv_ref[:, A_WIDTH:], xb)

    even_lane = (lax.broadcasted_iota(jnp.int32, (rows, A_WIDTH), 1) & A_HEAD_DIM) == 0
    k0_ref[...] = jnp.where(even_lane, k, placed[:, :A_WIDTH]).astype(BF16)
    k1_ref[...] = jnp.where(even_lane, placed[:, A_WIDTH:], k).astype(BF16)

    feature = lax.broadcasted_iota(jnp.int32, (A_WIDTH, rows), 0)
    even_row = (feature & A_HEAD_DIM) == 0
    slot = feature & (A_HEAD_DIM - 1)
    f_ones = jnp.where(slot < FOX_F_TERMS, 1.0, 0.0)
    q0_ref[...] = jnp.where(even_row, q_t, f_ones).astype(BF16)
    q1_ref[...] = jnp.where(even_row, f_ones, q_t).astype(BF16)
    v0_ref[...] = jnp.where(even_row, v_t, 1.0).astype(BF16)
    v1_ref[...] = jnp.where(even_row, 1.0, v_t).astype(BF16)


def _fox_proj_call(x2, w_k, w_qv, w_f, layer, fb, batch, seq, rows):
    T = x2.shape[0]
    nr = seq // rows
    tri = jnp.asarray(np.tril(np.ones((rows, rows), np.float32)), BF16)
    place = jnp.asarray(_fox_placement(), BF16)
    row_major = jax.ShapeDtypeStruct((T, A_WIDTH), BF16)
    transposed = jax.ShapeDtypeStruct((batch, A_WIDTH, seq), BF16)
    row_block = lambda width: pl.BlockSpec((rows, width), lambda b, r: (b * nr + r, 0))
    col_block = pl.BlockSpec((None, A_WIDTH, rows), lambda b, r: (b, 0, r))
    return pl.pallas_call(
        _fox_proj_kernel,
        grid=(batch, nr),
        in_specs=[
            row_block(D_MODEL),
            _layer_weight(layer, D_MODEL, A_WIDTH, 2),
            _layer_weight(layer, D_MODEL, 2 * A_WIDTH, 2),
            _layer_weight(layer, D_MODEL, LANES, 2),
            pl.BlockSpec((1, LANES), lambda b, r: (0, 0)),
            pl.BlockSpec((rows, rows), lambda b, r: (0, 0)),
            pl.BlockSpec(place.shape, lambda b, r: (0, 0)),
        ],
        out_specs=[col_block, col_block, row_block(A_WIDTH), row_block(A_WIDTH), col_block, col_block],
        out_shape=[transposed, transposed, row_major, row_major, transposed, transposed],
        scratch_shapes=[pltpu.VMEM((1, LANES), F32)],
        compiler_params=_params(("arbitrary", "arbitrary")),
        name="fox_proj",
    )(x2, w_k, w_qv, w_f, fb, tri, place)


def _fox_attn_kernel(q0_ref, q1_ref, k0_ref, k1_ref, v0_ref, v1_ref, o_ref,
                     x_ref, xmax_ref, m_ref, acc_ref, *, tile):
    seq = o_ref.shape[0]
    nq = seq // tile
    total = nq * (nq + 1) // 2
    key_pos = lax.broadcasted_iota(jnp.int32, (tile, UNIT_Q), 0)
    qry_pos = lax.broadcasted_iota(jnp.int32, (tile, UNIT_Q), 1)
    q_refs = (q0_ref, q1_ref)
    k_refs = (k0_ref, k1_ref)
    v_refs = (v0_ref, v1_ref)

    n_sub = tile // UNIT_Q
    n_heads = 2 * FOX_PAIRS_PER_STEP
    units = [(h, u) for h in range(n_heads) for u in range(n_sub)]

    def pair_block(h):
        return slice((h // 2) * LANES, (h // 2 + 1) * LANES)

    def scores(i, j, slot, masked, unit):
        h, u = unit
        qc = pl.multiple_of(i * tile + u * UNIT_Q, UNIT_Q)
        kc = pl.multiple_of(j * tile, tile)
        x = _dot(k_refs[h % 2][pl.ds(kc, tile), pair_block(h)], q_refs[h % 2][pair_block(h), pl.ds(qc, UNIT_Q)])
        if masked:
            x = jnp.where(key_pos <= qry_pos + u * UNIT_Q, x, NEG_BIG)
        x_ref[slot, h, u] = x
        xmax_ref[slot, h, u] = jnp.max(x, axis=0, keepdims=True)

    def accumulate(j, slot, unit):
        h, u = unit
        kc = pl.multiple_of(j * tile, tile)
        m = m_ref[h, u]
        m_new = jnp.maximum(m, xmax_ref[slot, h, u])
        p = jnp.exp2((x_ref[slot, h, u] - m_new).astype(BF16))
        acc_ref[h, u] = (jnp.exp2(m - m_new) * acc_ref[h, u]
                         + _dot(v_refs[h % 2][pair_block(h), pl.ds(kc, tile)], p))
        m_ref[h, u] = m_new

    def reset():
        m_ref[...] = jnp.full(m_ref.shape, NEG_BIG, F32)
        acc_ref[...] = jnp.zeros(acc_ref.shape, F32)

    def finalize(i):
        qc = pl.multiple_of(i * tile, tile)
        for pair in range(FOX_PAIRS_PER_STEP):
            a0 = jnp.concatenate([acc_ref[2 * pair, u] for u in range(n_sub)], axis=1)
            a1 = jnp.concatenate([acc_ref[2 * pair + 1, u] for u in range(n_sub)], axis=1)
            out_t = jnp.concatenate([a0[:A_HEAD_DIM] / a0[A_HEAD_DIM:A_HEAD_DIM + 1],
                                     a1[A_HEAD_DIM:] / a1[0:1]], axis=0)
            o_ref[pl.ds(qc, tile), pair * LANES:(pair + 1) * LANES] = out_t.T.astype(BF16)
        reset()

    def advance(i, j):
        wrap = j >= i
        return jnp.where(wrap, i + 1, i), jnp.where(wrap, 0, j + 1)

    def half(t, i, j, slot):
        ni, nj = advance(i, j)
        ci = jnp.minimum(ni, nq - 1)
        cj = jnp.minimum(nj, ci)
        valid = t < total

        cases = [(False, False), (True, False), (False, True)] + ([(True, True)] if nq == 1 else [])
        for next_diagonal, diagonal in cases:
            @pl.when(valid & ((cj == ci) == next_diagonal) & ((j == i) == diagonal))
            def _(next_diagonal=next_diagonal, diagonal=diagonal):
                for unit in units:
                    scores(ci, cj, 1 - slot, next_diagonal, unit)
                    accumulate(j, slot, unit)
                if diagonal:
                    finalize(i)

        return ni, nj

    def body(u, carry):
        i, j = carry
        i, j = half(2 * u, i, j, 0)
        return half(2 * u + 1, i, j, 1)

    reset()
    for unit in units:
        scores(0, 0, 0, True, unit)
    lax.fori_loop(0, (total + 1) // 2, body, (jnp.int32(0), jnp.int32(0)))


def _fox_attn_call(q0, q1, k0, k1, v0, v1, batch, seq, tile):
    T = batch * seq
    steps = A_HEADS // (2 * FOX_PAIRS_PER_STEP)
    width = FOX_PAIRS_PER_STEP * LANES
    n_heads = 2 * FOX_PAIRS_PER_STEP
    row_major = pl.BlockSpec((seq, width), lambda b, p: (b, p))
    transposed = pl.BlockSpec((None, width, seq), lambda b, p: (b, p, 0))
    n_sub = tile // UNIT_Q
    return pl.pallas_call(
        functools.partial(_fox_attn_kernel, tile=tile),
        grid=(batch, steps),
        in_specs=[transposed, transposed, row_major, row_major, transposed, transposed],
        out_specs=row_major,
        out_shape=jax.ShapeDtypeStruct((T, A_WIDTH), BF16),
        scratch_shapes=[
            pltpu.VMEM((2, n_heads, n_sub, tile, UNIT_Q), F32),
            pltpu.VMEM((2, n_heads, n_sub, 1, UNIT_Q), F32),
            pltpu.VMEM((n_heads, n_sub, 1, UNIT_Q), F32),
            pltpu.VMEM((n_heads, n_sub, LANES, UNIT_Q), F32),
        ],
        compiler_params=_params(("parallel", "parallel")),
        name="fox_attn",
    )(q0, q1, k0, k1, v0, v1)


def _merge_kernel(x_ref, yg_ref, yr_ref, ya_ref, wgate_ref, wpg_ref, wpr_ref, wpa_ref, wo_ref,
                  g_ref, b_ref, o_ref, *, alpha):
    half = x_ref.shape[0] // 2
    for s in range(2):
        rs = slice(s * half, (s + 1) * half)
        x = x_ref[rs, :]
        xb = x.astype(BF16)
        mixed = None
        for n, (y_ref, wp_ref) in enumerate(((yg_ref, wpg_ref), (yr_ref, wpr_ref), (ya_ref, wpa_ref))):
            gate = _sigmoid(_dot(xb, wgate_ref[:, n * D_MODEL:(n + 1) * D_MODEL]))
            term = gate * _dot(y_ref[rs, :], wp_ref[...])
            mixed = term if mixed is None else mixed + term
        out = _dot(mixed.astype(BF16), wo_ref[...])
        o_ref[rs, :] = _layer_norm(alpha * x + out, g_ref[...], b_ref[...])


def _merge_call(x2, yg, yr, ya, w_gate, w_pg, w_pr, w_pa, w_o, layer, ln_g, ln_b, alpha, tm):
    T = x2.shape[0]
    rows = lambda width: pl.BlockSpec((tm, width), lambda i: (i, 0))
    weight = lambda r, c: _layer_weight(layer, r, c, 1)
    vec = pl.BlockSpec((1, D_MODEL), lambda i: (0, 0))
    return pl.pallas_call(
        functools.partial(_merge_kernel, alpha=alpha),
        grid=(T // tm,),
        in_specs=[
            rows(D_MODEL), rows(G_WIDTH), rows(R_WIDTH), rows(A_WIDTH),
            weight(D_MODEL, N_BRANCH * D_MODEL),
            weight(G_WIDTH, D_MODEL), weight(R_WIDTH, D_MODEL), weight(A_WIDTH, D_MODEL),
            weight(D_MODEL, D_MODEL), vec, vec,
        ],
        out_specs=rows(D_MODEL),
        out_shape=jax.ShapeDtypeStruct((T, D_MODEL), F32),
        compiler_params=_params(("parallel",)),
        name="merge",
    )(x2, yg, yr, ya, w_gate, w_pg, w_pr, w_pa, w_o, ln_g, ln_b)


def _ffn_kernel(x_ref, wg_ref, wu_ref, wd_ref, g_ref, b_ref, o_ref, *, alpha):
    half = x_ref.shape[0] // 2
    halves = [slice(s * half, (s + 1) * half) for s in range(2)]
    hidden = []
    for rs in halves:
        xb = x_ref[rs, :].astype(BF16)
        hg = _dot(xb, wg_ref[...])
        hu = _dot(xb, wu_ref[...])
        hidden.append((hg * _sigmoid(hg) * hu).astype(BF16))
    for rs, hid in zip(halves, hidden):
        out = _dot(hid, wd_ref[...])
        o_ref[rs, :] = _layer_norm(alpha * x_ref[rs, :] + out, g_ref[...], b_ref[...])


def _ffn_call(x2, w_gate, w_up, w_down, layer, ln_g, ln_b, alpha, tm):
    T = x2.shape[0]
    rows = pl.BlockSpec((tm, D_MODEL), lambda i: (i, 0))
    vec = pl.BlockSpec((1, D_MODEL), lambda i: (0, 0))
    return pl.pallas_call(
        functools.partial(_ffn_kernel, alpha=alpha),
        grid=(T // tm,),
        in_specs=[
            rows,
            _layer_weight(layer, D_MODEL, D_FF, 1),
            _layer_weight(layer, D_MODEL, D_FF, 1),
            _layer_weight(layer, D_FF, D_MODEL, 1),
            vec, vec,
        ],
        out_specs=rows,
        out_shape=jax.ShapeDtypeStruct((T, D_MODEL), F32),
        compiler_params=_params(("parallel",)),
        name="ffn",
    )(x2, w_gate, w_up, w_down, ln_g, ln_b)


def kernel(x, w_in, gmlp_ln_g, gmlp_ln_b, gmlp_ws, gmlp_bs, hgrn_lb_logits, hgrn_norm_w, fox_fb,
           w_pg, w_pr, w_pa, w_o, ln1_g, ln1_b, w_gate, w_up, w_down, ln2_g, ln2_b):
    batch, seq, _ = x.shape
    depth = w_in.shape[0]
    T = batch * seq
    alpha = float((2 * depth) ** 0.25)
    rows = min(512, seq)
    tq = min(512, seq)
    tm = 512

    o_r = 2 * G_WIDTH
    o_a = o_r + 4 * R_WIDTH
    o_f = o_a + 3 * A_WIDTH
    o_gate = o_f + A_HEADS
    w_uv = w_in[:, :, :o_r].astype(BF16)
    w_r = w_in[:, :, o_r:o_a].astype(BF16)
    w_k = w_in[:, :, o_a + A_WIDTH:o_a + 2 * A_WIDTH].astype(BF16)
    w_qv = jnp.concatenate([w_in[:, :, o_a:o_a + A_WIDTH], w_in[:, :, o_a + 2 * A_WIDTH:o_f]],
                           axis=2).astype(BF16)
    spare = LANES - FOX_F_TERMS * A_HEADS
    w_f = jnp.pad(jnp.tile(w_in[:, :, o_f:o_gate], (1, 1, FOX_F_TERMS)), ((0, 0), (0, 0), (0, spare))).astype(BF16)
    w_bgate = w_in[:, :, o_gate:].astype(BF16)
    w_pg_b, w_pr_b, w_pa_b, w_o_b = (w.astype(BF16) for w in (w_pg, w_pr, w_pa, w_o))
    w_gate_b, w_up_b, w_down_b = (w.astype(BF16) for w in (w_gate, w_up, w_down))

    probs = jax.nn.softmax(hgrn_lb_logits.astype(F32), axis=0)
    lbs = jnp.cumsum(probs, axis=0) - probs[0:1]
    fb_pad = jnp.pad(jnp.tile(fox_fb.astype(F32), (1, FOX_F_TERMS)), ((0, 0), (0, spare)))

    x2 = x.reshape(T, D_MODEL)
    for l in range(depth):
        bs_b = jnp.broadcast_to(gmlp_bs[l][:, :, None], (G_GROUPS, G_CHUNK, LANES))
        y_g = _gmlp_call(x2, w_uv, l, gmlp_ln_g[l][None], gmlp_ln_b[l][None], gmlp_ws[l], bs_b, 2 * tm)
        y_r = _hgrn_call(x2, w_r, l, lbs[l][None], hgrn_norm_w[l][None], batch, seq, rows)
        q0, q1, k0, k1, v0, v1 = _fox_proj_call(x2, w_k, w_qv, w_f, l, fb_pad[l][None], batch, seq, rows)
        y_a = _fox_attn_call(q0, q1, k0, k1, v0, v1, batch, seq, tq)
        x2 = _merge_call(x2, y_g, y_r, y_a, w_bgate, w_pg_b, w_pr_b, w_pa_b, w_o_b, l,
                         ln1_g[l][None], ln1_b[l][None], alpha, tm)
        x2 = _ffn_call(x2, w_gate_b, w_up_b, w_down_b, l, ln2_g[l][None], ln2_b[l][None], alpha, tm)
    return x2.reshape(batch, seq, D_MODEL)
```

```python
import functools

import numpy as np
import jax
import jax.numpy as jnp
from jax import lax
from jax.experimental import pallas as pl
from jax.experimental.pallas import tpu as pltpu

BF16 = jnp.bfloat16
F32 = jnp.float32

D_MODEL = 1024
G_GROUPS = 4
G_WIDTH = 512
G_CHUNK = 128
R_HEADS = 4
R_DK = 128
R_WIDTH = 512
A_HEADS = 8
A_HEAD_DIM = 64
A_WIDTH = 512
N_BRANCH = 3
D_FF = 2816
LN_EPS = 1e-5
RMS_EPS = 1e-6
NEG_BIG = -1e30
LB_FLOOR = 1e-30

LOG2E = float(np.log2(np.e))
LANES = 128
SUBLANES = 8
HGRN_CHUNK = 128
HGRN_LEVELS = 7
VMEM_LIMIT_BYTES = 56 * 1024 * 1024


def _dot(a, b):
    return jnp.dot(a, b, preferred_element_type=F32)


def _dot_nt(a, b):
    return lax.dot_general(a, b, (((1,), (1,)), ((), ())), preferred_element_type=F32)


def _dot_tn_nt(w, x):
    return lax.dot_general(w, x, (((0,), (1,)), ((), ())), preferred_element_type=F32)


def _sigmoid(z):
    return 1.0 / (1.0 + jnp.exp(-z))


def _log_sigmoid(z):
    return jnp.minimum(z, 0.0) - jnp.log(1.0 + jnp.exp(-jnp.abs(z)))


def _gelu(z):
    return 0.5 * z * (1.0 + lax.erf(z * np.float32(2.0 ** -0.5)))


def _layer_norm(r, g, b):
    mu = jnp.mean(r, axis=-1, keepdims=True)
    d = r - mu
    var = jnp.mean(d * d, axis=-1, keepdims=True)
    return d * lax.rsqrt(var + LN_EPS) * g + b


def _split_bf16(a, terms):
    parts = []
    rem = a
    for _ in range(terms):
        p = rem.astype(BF16)
        parts.append(p)
        rem = rem - p.astype(F32)
    return parts


def _layer_weight(layer, rows, cols, grid_rank):
    index_map = (lambda i: (layer, 0, 0)) if grid_rank == 1 else (lambda b, r: (layer, 0, 0))
    return pl.BlockSpec((None, rows, cols), index_map, pipeline_mode=pl.Buffered(1))


def _params(sem):
    return pltpu.CompilerParams(dimension_semantics=sem, vmem_limit_bytes=VMEM_LIMIT_BYTES)


def _gmlp_kernel(x_ref, w_ref, lng_ref, lnb_ref, ws_ref, bs_ref, o_ref):
    tm = x_ref.shape[0]
    h = _dot(x_ref[...].astype(BF16), w_ref[...])
    u = _gelu(h[:, :G_WIDTH])
    v = _layer_norm(_gelu(h[:, G_WIDTH:]), lng_ref[...], lnb_ref[...]).astype(BF16)
    row = lax.broadcasted_iota(jnp.int32, (G_CHUNK, G_CHUNK), 0)
    col = lax.broadcasted_iota(jnp.int32, (G_CHUNK, G_CHUNK), 1)
    for g in range(G_GROUPS):
        w_causal = jnp.where(row >= col, ws_ref[g], 0.0).astype(BF16)
        cs = slice(g * LANES, (g + 1) * LANES)
        for n in range(tm // G_CHUNK):
            rs = slice(n * G_CHUNK, (n + 1) * G_CHUNK)
            sv = _dot(w_causal, v[rs, cs]) + bs_ref[g]
            o_ref[rs, cs] = (u[rs, cs] * sv).astype(BF16)


def _gmlp_call(x2, w_uv, layer, ln_g, ln_b, ws, bs_b, tm):
    T = x2.shape[0]
    return pl.pallas_call(
        _gmlp_kernel,
        grid=(T // tm,),
        in_specs=[
            pl.BlockSpec((tm, D_MODEL), lambda i: (i, 0)),
            _layer_weight(layer, D_MODEL, 2 * G_WIDTH, 1),
            pl.BlockSpec((1, G_WIDTH), lambda i: (0, 0)),
            pl.BlockSpec((1, G_WIDTH), lambda i: (0, 0)),
            pl.BlockSpec((G_GROUPS, G_CHUNK, G_CHUNK), lambda i: (0, 0, 0)),
            pl.BlockSpec((G_GROUPS, G_CHUNK, LANES), lambda i: (0, 0, 0)),
        ],
        out_specs=pl.BlockSpec((tm, G_WIDTH), lambda i: (i, 0)),
        out_shape=jax.ShapeDtypeStruct((T, G_WIDTH), BF16),
        compiler_params=_params(("parallel",)),
        name="gmlp",
    )(x2, w_uv, ln_g, ln_b, ws, bs_b)


def _hgrn_constants():
    c = HGRN_CHUNK
    t = np.arange(c)[:, None]
    tau = np.arange(c)[None, :]
    prefix = (tau <= t).astype(np.float32)
    xor = np.maximum(t ^ tau, 1)
    level_map = np.where(tau < t, np.floor(np.log2(xor)).astype(np.int32),
                         np.where(tau == t, -1, -2)).astype(np.int32)
    return prefix, level_map


def _hgrn_kernel(x_ref, w_ref, lb_ref, nw_ref, tri_ref, lev_ref, o_ref, proj_ref, gate_ref, state_ref):
    rows = x_ref.shape[0]
    c = HGRN_CHUNK

    @pl.when(pl.program_id(1) == 0)
    def _():
        state_ref[...] = jnp.zeros_like(state_ref)

    proj_ref[...] = _dot(x_ref[...].astype(BF16), w_ref[...])
    lb = lb_ref[...]
    log_lb = jnp.log(jnp.maximum(lb, LB_FLOOR))
    log_1m_lb = jnp.log1p(-lb)
    one_m_lb = 1.0 - lb
    norm_w = nw_ref[...]

    def odd_blocks(a, width):
        return jnp.concatenate([a[(2 * g + 1) * width:(2 * g + 2) * width] for g in range(c // (2 * width))], axis=0)

    def gates(ci):
        rs = slice(ci * c, (ci + 1) * c)
        z = proj_ref[rs, R_WIDTH:2 * R_WIDTH]
        b = log_1m_lb + _log_sigmoid(z)
        lf = jnp.maximum(log_lb, b) + jnp.log(1.0 + jnp.exp(-jnp.abs(log_lb - b)))
        gate_ref[1, rs, :] = one_m_lb * _sigmoid(-z)
        lf = lf * np.float32(LOG2E)
        gate_ref[0, rs, :] = lf
        lf_hi, lf_lo = _split_bf16(lf, 2)
        tri = tri_ref[...]
        gate_ref[2, rs, :] = _dot(tri, lf_hi) + _dot(tri, lf_lo)

    def chunk(ci):
        rs = slice(ci * c, (ci + 1) * c)
        lev = lev_ref[...]
        lf = gate_ref.at[0, rs, :]
        key = gate_ref.at[1, rs, :]
        cum_all = gate_ref.at[2, rs, :]
        for h in range(R_HEADS):
            hs = slice(h * LANES, (h + 1) * LANES)
            q = proj_ref[rs, h * LANES:(h + 1) * LANES]
            kh = key[:, hs]
            val = proj_ref[rs, 2 * R_WIDTH + h * LANES:2 * R_WIDTH + (h + 1) * LANES]
            gate = proj_ref[rs, 3 * R_WIDTH + h * LANES:3 * R_WIDTH + (h + 1) * LANES]
            q_b = q.astype(BF16)
            k_b = kh.astype(BF16)
            cum = cum_all[:, hs]

            def level_decay(level, cum=cum, hs=hs):
                if level == 0:
                    return lf[:, hs]
                width = 1 << level
                pieces = []
                for g in range(c // (2 * width)):
                    base = 2 * g * width
                    boundary = cum[base + width - 1:base + width]
                    pieces.append(boundary - cum[base:base + width])
                    pieces.append(cum[base + width:base + 2 * width] - boundary)
                return jnp.concatenate(pieces, axis=0)

            scores = jnp.zeros((c, c), F32)
            for level in range(HGRN_LEVELS):
                width = 1 << level
                decay = jnp.exp2(level_decay(level))
                decay_b = decay.astype(BF16)
                k_l = k_b * decay_b if level else k_b
                if width < SUBLANES:
                    s_l = _dot_nt(q_b * decay_b, k_l)
                    scores = jnp.where(lev == level, s_l, scores)
                else:
                    q_l = (odd_blocks(q, width) * odd_blocks(decay, width)).astype(BF16)
                    s_l = _dot_nt(q_l, k_l)
                    rows_out = []
                    for g in range(c // (2 * width)):
                        base = 2 * g * width
                        odd = slice(base + width, base + 2 * width)
                        rows_out.append(scores[base:base + width])
                        rows_out.append(jnp.where(lev[odd] == level, s_l[g * width:(g + 1) * width], scores[odd]))
                    scores = jnp.concatenate(rows_out, axis=0)
            diag = jnp.sum(q * kh, axis=-1, keepdims=True)
            scores = jnp.where(lev == -1, diag, scores)
            val_b = val.astype(BF16)
            intra = _dot(scores.astype(BF16), val_b)
            e_in = jnp.exp2(cum)
            e_out = jnp.exp2(cum[c - 1:c] - cum)
            state = state_ref[h]
            inter = _dot_nt((q * e_in).astype(BF16), state.astype(BF16))
            e_last = e_in[c - 1:c, :]
            state_ref[h] = state * e_last + _dot(val.T.astype(BF16), (kh * e_out).astype(BF16))
            o = inter + intra
            o = o * lax.rsqrt(jnp.mean(o * o, axis=-1, keepdims=True) + RMS_EPS)
            o = o * norm_w * (gate * _sigmoid(gate))
            o_ref[rs, h * LANES:(h + 1) * LANES] = o.astype(BF16)

    for ci in range(rows // c):
        gates(ci)
        chunk(ci)


def _hgrn_call(x2, w_r, layer, lb, norm_w, batch, seq, rows):
    T = x2.shape[0]
    nr = seq // rows
    prefix, level_map = _hgrn_constants()
    return pl.pallas_call(
        _hgrn_kernel,
        grid=(batch, nr),
        in_specs=[
            pl.BlockSpec((rows, D_MODEL), lambda b, r: (b * nr + r, 0)),
            _layer_weight(layer, D_MODEL, 4 * R_WIDTH, 2),
            pl.BlockSpec((1, R_WIDTH), lambda b, r: (0, 0)),
            pl.BlockSpec((1, R_DK), lambda b, r: (0, 0)),
            pl.BlockSpec((HGRN_CHUNK, HGRN_CHUNK), lambda b, r: (0, 0)),
            pl.BlockSpec((HGRN_CHUNK, HGRN_CHUNK), lambda b, r: (0, 0)),
        ],
        out_specs=pl.BlockSpec((rows, R_WIDTH), lambda b, r: (b * nr + r, 0)),
        out_shape=jax.ShapeDtypeStruct((T, R_WIDTH), BF16),
        scratch_shapes=[
            pltpu.VMEM((rows, 4 * R_WIDTH), F32),
            pltpu.VMEM((3, rows, R_WIDTH), F32),
            pltpu.VMEM((R_HEADS, R_DK, R_DK), F32),
        ],
        compiler_params=_params(("arbitrary", "arbitrary")),
        name="hgrn",
    )(x2, w_r, lb, norm_w, jnp.asarray(prefix, BF16), jnp.asarray(level_map))


FOX_F_TERMS = 3
FOX_PAIRS_PER_STEP = 2
UNIT_Q = 512


def _fox_placement():
    place = np.zeros((LANES, 2 * A_WIDTH), np.float32)
    for h in range(A_HEADS):
        pair, odd = divmod(h, 2)
        spare0 = pair * LANES + (0 if odd else A_HEAD_DIM)
        for t in range(FOX_F_TERMS):
            place[t * A_HEADS + h, odd * A_WIDTH + spare0 + t] = 1.0
    return place


def _fox_pack_pieces(a, lane):
    packed = jnp.zeros(a.shape, F32)
    for t, piece in reversed(list(enumerate(_split_bf16(a, FOX_F_TERMS)))):
        packed = jnp.where(lane < (t + 1) * A_HEADS, piece.astype(F32), packed)
    return packed.astype(BF16)


def _fox_proj_kernel(x_ref, wk_ref, wqv_ref, wf_ref, fb_ref, tri_ref, place_ref,
                     q0_ref, q1_ref, k0_ref, k1_ref, v0_ref, v1_ref, carry_ref):
    rows = x_ref.shape[0]

    @pl.when(pl.program_id(1) == 0)
    def _():
        carry_ref[...] = jnp.zeros_like(carry_ref)

    xb = x_ref[...].astype(BF16)

    lane = lax.broadcasted_iota(jnp.int32, (rows, LANES), 1)
    log_f = _log_sigmoid(_dot(xb, wf_ref[...]) + fb_ref[...])
    k = _dot(xb, wk_ref[...])
    part = _dot(tri_ref[...], _fox_pack_pieces(log_f, lane))
    q_t = _dot_tn_nt(wqv_ref[:, :A_WIDTH], xb) * np.float32(A_HEAD_DIM ** -0.5 * LOG2E)
    cum = part
    for t in range(1, FOX_F_TERMS):
        cum = cum + pltpu.roll(part, t * A_HEADS, axis=1) + pltpu.roll(part, LANES - t * A_HEADS, axis=1)
    cum = cum + carry_ref[...]
    carry_ref[...] = cum[rows - 1:rows, :]
    placed = _dot(_fox_pack_pieces(cum * np.float32(-LOG2E), lane), place_ref[...])
    v_t = _dot_tn_nt(wqv_ref[:, A_WIDTH:], xb)

    even_lane = (lax.broadcasted_iota(jnp.int32, (rows, A_WIDTH), 1) & A_HEAD_DIM) == 0
    k0_ref[...] = jnp.where(even_lane, k, placed[:, :A_WIDTH]).astype(BF16)
    k1_ref[...] = jnp.where(even_lane, placed[:, A_WIDTH:], k).astype(BF16)

    feature = lax.broadcasted_iota(jnp.int32, (A_WIDTH, rows), 0)
    even_row = (feature & A_HEAD_DIM) == 0
    slot = feature & (A_HEAD_DIM - 1)
    f_ones = jnp.where(slot < FOX_F_TERMS, 1.0, 0.0)
    q0_ref[...] = jnp.where(even_row, q_t, f_ones).astype(BF16)
    q1_ref[...] = jnp.where(even_row, f_ones, q_t).astype(BF16)
    v0_ref[...] = jnp.where(even_row, v_t, 1.0).astype(BF16)
    v1_ref[...] = jnp.where(even_row, 1.0, v_t).astype(BF16)


def _fox_proj_call(x2, w_k, w_qv, w_f, layer, fb, batch, seq, rows):
    T = x2.shape[0]
    nr = seq // rows
    tri = jnp.asarray(np.tril(np.ones((rows, rows), np.float32)), BF16)
    place = jnp.asarray(_fox_placement(), BF16)
    row_major = jax.ShapeDtypeStruct((T, A_WIDTH), BF16)
    transposed = jax.ShapeDtypeStruct((batch, A_WIDTH, seq), BF16)
    row_block = lambda width: pl.BlockSpec((rows, width), lambda b, r: (b * nr + r, 0))
    col_block = pl.BlockSpec((None, A_WIDTH, rows), lambda b, r: (b, 0, r))
    return pl.pallas_call(
        _fox_proj_kernel,
        grid=(batch, nr),
        in_specs=[
            row_block(D_MODEL),
            _layer_weight(layer, D_MODEL, A_WIDTH, 2),
            _layer_weight(layer, D_MODEL, 2 * A_WIDTH, 2),
            _layer_weight(layer, D_MODEL, LANES, 2),
            pl.BlockSpec((1, LANES), lambda b, r: (0, 0)),
            pl.BlockSpec((rows, rows), lambda b, r: (0, 0)),
            pl.BlockSpec(place.shape, lambda b, r: (0, 0)),
        ],
        out_specs=[col_block, col_block, row_block(A_WIDTH), row_block(A_WIDTH), col_block, col_block],
        out_shape=[transposed, transposed, row_major, row_major, transposed, transposed],
        scratch_shapes=[pltpu.VMEM((1, LANES), F32)],
        compiler_params=_params(("arbitrary", "arbitrary")),
        name="fox_proj",
    )(x2, w_k, w_qv, w_f, fb, tri, place)


def _fox_attn_kernel(q0_ref, q1_ref, k0_ref, k1_ref, v0_ref, v1_ref, o_ref,
                     x_ref, xmax_ref, m_ref, acc_ref, *, tile):
    seq = o_ref.shape[0]
    nq = seq // tile
    total = nq * (nq + 1) // 2
    key_pos = lax.broadcasted_iota(jnp.int32, (tile, UNIT_Q), 0)
    qry_pos = lax.broadcasted_iota(jnp.int32, (tile, UNIT_Q), 1)
    q_refs = (q0_ref, q1_ref)
    k_refs = (k0_ref, k1_ref)
    v_refs = (v0_ref, v1_ref)

    n_sub = tile // UNIT_Q
    n_heads = 2 * FOX_PAIRS_PER_STEP
    units = [(h, u) for h in range(n_heads) for u in range(n_sub)]

    def pair_block(h):
        return slice((h // 2) * LANES, (h // 2 + 1) * LANES)

    def scores(i, j, slot, masked, unit):
        h, u = unit
        qc = pl.multiple_of(i * tile + u * UNIT_Q, UNIT_Q)
        kc = pl.multiple_of(j * tile, tile)
        x = _dot(k_refs[h % 2][pl.ds(kc, tile), pair_block(h)], q_refs[h % 2][pair_block(h), pl.ds(qc, UNIT_Q)])
        if masked:
            x = jnp.where(key_pos <= qry_pos + u * UNIT_Q, x, NEG_BIG)
        x_ref[slot, h, u] = x
        xmax_ref[slot, h, u] = jnp.max(x, axis=0, keepdims=True)

    def accumulate(j, slot, unit):
        h, u = unit
        kc = pl.multiple_of(j * tile, tile)
        m = m_ref[h, u]
        m_new = jnp.maximum(m, xmax_ref[slot, h, u])
        p = jnp.exp2((x_ref[slot, h, u] - m_new).astype(BF16))
        acc_ref[h, u] = (jnp.exp2(m - m_new) * acc_ref[h, u]
                         + _dot(v_refs[h % 2][pair_block(h), pl.ds(kc, tile)], p))
        m_ref[h, u] = m_new

    def reset():
        m_ref[...] = jnp.full(m_ref.shape, NEG_BIG, F32)
        acc_ref[...] = jnp.zeros(acc_ref.shape, F32)

    def finalize(i):
        qc = pl.multiple_of(i * tile, tile)
        for pair in range(FOX_PAIRS_PER_STEP):
            a0 = jnp.concatenate([acc_ref[2 * pair, u] for u in range(n_sub)], axis=1)
            a1 = jnp.concatenate([acc_ref[2 * pair + 1, u] for u in range(n_sub)], axis=1)
            out_t = jnp.concatenate([a0[:A_HEAD_DIM] / a0[A_HEAD_DIM:A_HEAD_DIM + 1],
                                     a1[A_HEAD_DIM:] / a1[0:1]], axis=0)
            o_ref[pl.ds(qc, tile), pair * LANES:(pair + 1) * LANES] = out_t.T.astype(BF16)
        reset()

    def advance(i, j):
        wrap = j >= i
        return jnp.where(wrap, i + 1, i), jnp.where(wrap, 0, j + 1)

    def half(t, i, j, slot):
        ni, nj = advance(i, j)
        ci = jnp.minimum(ni, nq - 1)
        cj = jnp.minimum(nj, ci)
        valid = t < total

        cases = [(False, False), (True, False), (False, True)] + ([(True, True)] if nq == 1 else [])
        for next_diagonal, diagonal in cases:
            @pl.when(valid & ((cj == ci) == next_diagonal) & ((j == i) == diagonal))
            def _(next_diagonal=next_diagonal, diagonal=diagonal):
                for unit in units:
                    scores(ci, cj, 1 - slot, next_diagonal, unit)
                    accumulate(j, slot, unit)
                if diagonal:
                    finalize(i)

        return ni, nj

    def body(u, carry):
        i, j = carry
        i, j = half(2 * u, i, j, 0)
        return half(2 * u + 1, i, j, 1)

    reset()
    for unit in units:
        scores(0, 0, 0, True, unit)
    lax.fori_loop(0, (total + 1) // 2, body, (jnp.int32(0), jnp.int32(0)))


def _fox_attn_call(q0, q1, k0, k1, v0, v1, batch, seq, tile):
    T = batch * seq
    steps = A_HEADS // (2 * FOX_PAIRS_PER_STEP)
    width = FOX_PAIRS_PER_STEP * LANES
    n_heads = 2 * FOX_PAIRS_PER_STEP
    row_major = pl.BlockSpec((seq, width), lambda b, p: (b, p))
    transposed = pl.BlockSpec((None, width, seq), lambda b, p: (b, p, 0))
    n_sub = tile // UNIT_Q
    return pl.pallas_call(
        functools.partial(_fox_attn_kernel, tile=tile),
        grid=(batch, steps),
        in_specs=[transposed, transposed, row_major, row_major, transposed, transposed],
        out_specs=row_major,
        out_shape=jax.ShapeDtypeStruct((T, A_WIDTH), BF16),
        scratch_shapes=[
            pltpu.VMEM((2, n_heads, n_sub, tile, UNIT_Q), F32),
            pltpu.VMEM((2, n_heads, n_sub, 1, UNIT_Q), F32),
            pltpu.VMEM((n_heads, n_sub, 1, UNIT_Q), F32),
            pltpu.VMEM((n_heads, n_sub, LANES, UNIT_Q), F32),
        ],
        compiler_params=_params(("parallel", "parallel")),
        name="fox_attn",
    )(q0, q1, k0, k1, v0, v1)


def _merge_kernel(x_ref, yg_ref, yr_ref, ya_ref, wgate_ref, wpg_ref, wpr_ref, wpa_ref, wo_ref,
                  g_ref, b_ref, o_ref, *, alpha):
    half = x_ref.shape[0] // 2
    for s in range(2):
        rs = slice(s * half, (s + 1) * half)
        x = x_ref[rs, :]
        xb = x.astype(BF16)
        mixed = None
        for n, (y_ref, wp_ref) in enumerate(((yg_ref, wpg_ref), (yr_ref, wpr_ref), (ya_ref, wpa_ref))):
            gate = _sigmoid(_dot(xb, wgate_ref[:, n * D_MODEL:(n + 1) * D_MODEL]))
            term = gate * _dot(y_ref[rs, :], wp_ref[...])
            mixed = term if mixed is None else mixed + term
        out = _dot(mixed.astype(BF16), wo_ref[...])
        o_ref[rs, :] = _layer_norm(alpha * x + out, g_ref[...], b_ref[...])


def _merge_call(x2, yg, yr, ya, w_gate, w_pg, w_pr, w_pa, w_o, layer, ln_g, ln_b, alpha, tm):
    T = x2.shape[0]
    rows = lambda width: pl.BlockSpec((tm, width), lambda i: (i, 0))
    weight = lambda r, c: _layer_weight(layer, r, c, 1)
    vec = pl.BlockSpec((1, D_MODEL), lambda i: (0, 0))
    return pl.pallas_call(
        functools.partial(_merge_kernel, alpha=alpha),
        grid=(T // tm,),
        in_specs=[
            rows(D_MODEL), rows(G_WIDTH), rows(R_WIDTH), rows(A_WIDTH),
            weight(D_MODEL, N_BRANCH * D_MODEL),
            weight(G_WIDTH, D_MODEL), weight(R_WIDTH, D_MODEL), weight(A_WIDTH, D_MODEL),
            weight(D_MODEL, D_MODEL), vec, vec,
        ],
        out_specs=rows(D_MODEL),
        out_shape=jax.ShapeDtypeStruct((T, D_MODEL), F32),
        compiler_params=_params(("parallel",)),
        name="merge",
    )(x2, yg, yr, ya, w_gate, w_pg, w_pr, w_pa, w_o, ln_g, ln_b)


def _ffn_kernel(x_ref, wg_ref, wu_ref, wd_ref, g_ref, b_ref, o_ref, *, alpha):
    half = x_ref.shape[0] // 2
    halves = [slice(s * half, (s + 1) * half) for s in range(2)]
    hidden = []
    for rs in halves:
        xb = x_ref[rs, :].astype(BF16)
        hg = _dot(xb, wg_ref[...])
        hu = _dot(xb, wu_ref[...])
        hidden.append((hg * _sigmoid(hg) * hu).astype(BF16))
    for rs, hid in zip(halves, hidden):
        out = _dot(hid, wd_ref[...])
        o_ref[rs, :] = _layer_norm(alpha * x_ref[rs, :] + out, g_ref[...], b_ref[...])


def _ffn_call(x2, w_gate, w_up, w_down, layer, ln_g, ln_b, alpha, tm):
    T = x2.shape[0]
    rows = pl.BlockSpec((tm, D_MODEL), lambda i: (i, 0))
    vec = pl.BlockSpec((1, D_MODEL), lambda i: (0, 0))
    return pl.pallas_call(
        functools.partial(_ffn_kernel, alpha=alpha),
        grid=(T // tm,),
        in_specs=[
            rows,
            _layer_weight(layer, D_MODEL, D_FF, 1),
            _layer_weight(layer, D_MODEL, D_FF, 1),
            _layer_weight(layer, D_FF, D_MODEL, 1),
            vec, vec,
        ],
        out_specs=rows,
        out_shape=jax.ShapeDtypeStruct((T, D_MODEL), F32),
        compiler_params=_params(("parallel",)),
        name="ffn",
    )(x2, w_gate, w_up, w_down, ln_g, ln_b)


def kernel(x, w_in, gmlp_ln_g, gmlp_ln_b, gmlp_ws, gmlp_bs, hgrn_lb_logits, hgrn_norm_w, fox_fb,
           w_pg, w_pr, w_pa, w_o, ln1_g, ln1_b, w_gate, w_up, w_down, ln2_g, ln2_b):
    batch, seq, _ = x.shape
    depth = w_in.shape[0]
    T = batch * seq
    alpha = float((2 * depth) ** 0.25)
    rows = min(512, seq)
    tq = min(512, seq)
    tm = 512

    o_r = 2 * G_WIDTH
    o_a = o_r + 4 * R_WIDTH
    o_f = o_a + 3 * A_WIDTH
    o_gate = o_f + A_HEADS
    w_uv = w_in[:, :, :o_r].astype(BF16)
    w_r = w_in[:, :, o_r:o_a].astype(BF16)
    w_k = w_in[:, :, o_a + A_WIDTH:o_a + 2 * A_WIDTH].astype(BF16)
    w_qv = jnp.concatenate([w_in[:, :, o_a:o_a + A_WIDTH], w_in[:, :, o_a + 2 * A_WIDTH:o_f]],
                           axis=2).astype(BF16)
    spare = LANES - FOX_F_TERMS * A_HEADS
    w_f = jnp.pad(jnp.tile(w_in[:, :, o_f:o_gate], (1, 1, FOX_F_TERMS)), ((0, 0), (0, 0), (0, spare))).astype(BF16)
    w_bgate = w_in[:, :, o_gate:].astype(BF16)
    w_pg_b, w_pr_b, w_pa_b, w_o_b = (w.astype(BF16) for w in (w_pg, w_pr, w_pa, w_o))
    w_gate_b, w_up_b, w_down_b = (w.astype(BF16) for w in (w_gate, w_up, w_down))

    probs = jax.nn.softmax(hgrn_lb_logits.astype(F32), axis=0)
    lbs = jnp.cumsum(probs, axis=0) - probs[0:1]
    fb_pad = jnp.pad(jnp.tile(fox_fb.astype(F32), (1, FOX_F_TERMS)), ((0, 0), (0, spare)))

    x2 = x.reshape(T, D_MODEL)
    for l in range(depth):
        bs_b = jnp.broadcast_to(gmlp_bs[l][:, :, None], (G_GROUPS, G_CHUNK, LANES))
        y_g = _gmlp_call(x2, w_uv, l, gmlp_ln_g[l][None], gmlp_ln_b[l][None], gmlp_ws[l], bs_b, 2 * tm)
        y_r = _hgrn_call(x2, w_r, l, lbs[l][None], hgrn_norm_w[l][None], batch, seq, rows)
        q0, q1, k0, k1, v0, v1 = _fox_proj_call(x2, w_k, w_qv, w_f, l, fb_pad[l][None], batch, seq, rows)
        y_a = _fox_attn_call(q0, q1, k0, k1, v0, v1, batch, seq, tq)
        x2 = _merge_call(x2, y_g, y_r, y_a, w_bgate, w_pg_b, w_pr_b, w_pa_b, w_o_b, l,
                         ln1_g[l][None], ln1_b[l][None], alpha, tm)
        x2 = _ffn_call(x2, w_gate_b, w_up_b, w_down_b, l, ln2_g[l][None], ln2_b[l][None], alpha, tm)
    return x2.reshape(batch, seq, D_MODEL)
```
